```python
import math
import jax
import jax.numpy as jnp
from jax import lax
import numpy as np

D_MODEL = 2048
BATCH = 16
SEQ = 256
DEPTH = 4
DEC_BATCH = 8
DEC_SEQ = 4096
PAST_LEN = 256

GRID_W = 64
D_HALF = D_MODEL // 2
HEAD_DIM = 64
N_HEADS_A = D_HALF // HEAD_DIM
N_HEADS_B = D_HALF // HEAD_DIM
WIN_H = 8
WIN_W = 16
Q_BLK_W = 16
K_BLK_W = 32
Q_BLK = 128
LORA_W = 64
LORA_A = 64
D_C = D_MODEL
HY_EMB = 33
HY_BANDS = (HY_EMB - 1) // 2
HY_ORDER = 64
HY_TARGET = 1e-2
HY_FAST = 0.3
HY_SLOW = 1.5
N_EVEN = (DEPTH + 1) // 2
N_ODD = DEPTH // 2
E_COLS = 8 * D_HALF + 2 * LORA_W + 2 * LORA_A
ALPHA = (2 * DEPTH) ** 0.25
BETA = (8 * DEPTH) ** -0.25
LN_EPS = 1e-5
GN_EPS = 64e-5

kernel_name = 'hybrid_natten_rwkv7_hyena_step'


def layer_norm(x, g, b):
    xf = x.astype(jnp.float32)
    mu = jnp.mean(xf, -1, keepdims=True)
    var = jnp.mean(jnp.square(xf - mu), -1, keepdims=True)
    return ((xf - mu) * lax.rsqrt(var + LN_EPS) * g + b).astype(x.dtype)


def ada_mod(cond, w, b):
    m = jax.nn.silu(cond) @ w + b
    shift, scale, gate = jnp.split(m[:, None, :], 3, axis=-1)
    return shift, scale, gate


def conv3(x, w):
    L = x.shape[1]
    xp = jnp.pad(x, ((0, 0), (1, 1), (0, 0)))
    return xp[:, :L] * w[0] + xp[:, 1:L + 1] * w[1] + xp[:, 2:] * w[2]


def attn_dense(q, k, v):
    B, L, H, d = q.shape
    nb = L // Q_BLK
    scale = d ** -0.5
    qb = q.reshape(B, nb, Q_BLK, H, d).transpose(1, 0, 2, 3, 4)

    def blk(qi):
        s = jnp.einsum('bqhd,bkhd->bhqk', qi, k).astype(jnp.float32) * scale
        p = jax.nn.softmax(s, axis=-1).astype(v.dtype)
        return jnp.einsum('bhqk,bkhd->bqhd', p, v)

    o = lax.map(blk, qb)
    return o.transpose(1, 0, 2, 3, 4).reshape(B, L, H, d)


def na_latent(q, k, v, k_ctx, v_ctx, rpb):
    B, T, H, d = q.shape
    rows = T // GRID_W
    kh = min(WIN_H, rows)
    nqb = GRID_W // Q_BLK_W
    scale = d ** -0.5
    qg = q.reshape(B, rows, nqb, Q_BLK_W, H, d)
    kg = k.reshape(B, rows, GRID_W, H, d)
    vg = v.reshape(B, rows, GRID_W, H, d)
    qcol = jnp.arange(GRID_W).reshape(nqb, Q_BLK_W)
    kcol = jnp.clip(qcol[:, :1] - WIN_W // 2, 0, GRID_W - K_BLK_W) + jnp.arange(K_BLK_W)
    qstart = jnp.clip(qcol - WIN_W // 2, 0, GRID_W - WIN_W)
    kc = kcol[:, None, :]
    col_mask = (kc >= qstart[..., None]) & (kc < qstart[..., None] + WIN_W)
    col_idx = jnp.clip(kc - qcol[..., None], 1 - WIN_W, WIN_W - 1) + WIN_W - 1
    rpb_col = rpb[:, :, col_idx]
    neg = jnp.float32(-1e30)
    n_nb = kh * K_BLK_W

    def row_block(r):
        rs = jnp.clip(r - kh // 2, 0, rows - kh)
        k_rows = lax.dynamic_slice_in_dim(kg, rs, kh, axis=1)[:, :, kcol]
        v_rows = lax.dynamic_slice_in_dim(vg, rs, kh, axis=1)[:, :, kcol]
        q_r = lax.dynamic_index_in_dim(qg, r, axis=1, keepdims=False)
        s_nb = jnp.einsum('bnqhd,bxnkhd->bhnqxk', q_r, k_rows).astype(jnp.float32) * scale
        dr_idx = rs + jnp.arange(kh) - r + WIN_H - 1
        bias = rpb_col[:, dr_idx].transpose(0, 2, 3, 1, 4)
        s_nb = jnp.where(col_mask[:, :, None, :], s_nb + bias, neg)
        s_nb = s_nb.reshape(B, H, nqb, Q_BLK_W, n_nb)
        s_cx = jnp.einsum('bnqhd,bchd->bhnqc', q_r, k_ctx).astype(jnp.float32) * scale
        p = jax.nn.softmax(jnp.concatenate([s_nb, s_cx], axis=-1), axis=-1).astype(v.dtype)
        p_nb = p[..., :n_nb].reshape(B, H, nqb, Q_BLK_W, kh, K_BLK_W)
        p_cx = p[..., n_nb:]
        o = (jnp.einsum('bhnqxk,bxnkhd->bnqhd', p_nb, v_rows)
             + jnp.einsum('bhnqc,bchd->bnqhd', p_cx, v_ctx))
        return o.reshape(B, GRID_W, H, d)

    o = lax.map(row_block, jnp.arange(rows))
    return o.transpose(1, 0, 2, 3, 4).reshape(B, T, H, d)


def rwkv_scan(r, w, k, v, kk, a, S0, reverse):
    def step(S, inp):
        rt, wt, kt, vt, kkt, at = inp
        sa = jnp.einsum('bhij,bhj->bhi', S, -kkt)
        S = (S * wt[:, :, None, :] + sa[..., None] * (kkt * at)[:, :, None, :]
             + vt[..., None] * kt[:, :, None, :])
        return S, jnp.einsum('bhij,bhj->bhi', S, rt)

    S, o = lax.scan(step, S0, (r, w, k, v, kk, a), reverse=reverse)
    return o, S


def rwkv_branch(rkv, wdn, adn, w0, wup, a0, aup, k_k, k_a, r_k, gn_g, gn_b, S0):
    B, L, _ = rkv.shape
    f32 = jnp.float32
    r, k, v = jnp.split(rkv.astype(f32), 3, axis=-1)
    wdn = wdn.astype(f32).reshape(B, L, 2, LORA_W)
    adn = adn.astype(f32).reshape(B, L, 2, LORA_A)
    w_raw = -jax.nn.softplus(-(w0 + jnp.einsum('bldr,drc->bldc', jnp.tanh(wdn), wup))) - 0.5
    decay = jnp.exp(-jnp.exp(w_raw))
    a = jax.nn.sigmoid(a0 + jnp.einsum('bldr,drc->bldc', adn, aup))
    kd = k[:, :, None] * (1.0 + (a - 1.0) * k_a)
    kk = (k * k_k).reshape(B, L, N_HEADS_B, HEAD_DIM)
    kk = kk / jnp.maximum(jnp.sqrt(jnp.sum(jnp.square(kk), -1, keepdims=True)), 1e-12)

    def sc(t):
        return t.reshape(B, L, N_HEADS_B, HEAD_DIM).transpose(1, 0, 2, 3)

    kk_s = kk.transpose(1, 0, 2, 3)
    r_s, v_s = sc(r), sc(v)
    S0 = S0.astype(f32)
    o_f, S_f = rwkv_scan(r_s, sc(decay[:, :, 0]), sc(kd[:, :, 0]), v_s, kk_s, sc(a[:, :, 0]), S0[:, 0], False)
    o_b, S_b = rwkv_scan(r_s, sc(decay[:, :, 1]), sc(kd[:, :, 1]), v_s, kk_s, sc(a[:, :, 1]), S0[:, 1], True)
    o = (o_f + o_b).transpose(1, 0, 2, 3)
    mu = jnp.mean(o, -1, keepdims=True)
    var = jnp.mean(jnp.square(o - mu), -1, keepdims=True)
    o = ((o - mu) * lax.rsqrt(var + GN_EPS)).reshape(B, L, D_HALF) * gn_g + gn_b
    rh = r.reshape(B, L, N_HEADS_B, HEAD_DIM)
    vh = v.reshape(B, L, N_HEADS_B, HEAD_DIM)
    kdh = (kd[:, :, 0] + kd[:, :, 1]).reshape(B, L, N_HEADS_B, HEAD_DIM)
    bonus = jnp.sum(rh * kdh * r_k, -1, keepdims=True) * vh
    out = (o + bonus.reshape(B, L, D_HALF)).astype(rkv.dtype)
    return out, jnp.stack([S_f, S_b], axis=1).astype(rkv.dtype)


def even_mixer(h, w_in, w_out, rpb, conv_w, w0, wup, a0, aup, k_k, k_a, r_k, gn_g, gn_b, kv_ctx, S0):
    B, L, _ = h.shape
    z = h @ w_in
    qa = z[..., 0 * D_HALF:1 * D_HALF].reshape(B, L, N_HEADS_A, HEAD_DIM)
    ka = z[..., 1 * D_HALF:2 * D_HALF].reshape(B, L, N_HEADS_A, HEAD_DIM)
    va = z[..., 2 * D_HALF:3 * D_HALF].reshape(B, L, N_HEADS_A, HEAD_DIM)
    ga = z[..., 3 * D_HALF:4 * D_HALF]
    rkv = conv3(z[..., 4 * D_HALF:7 * D_HALF], conv_w)
    gb = z[..., 7 * D_HALF:8 * D_HALF]
    wdn = z[..., 8 * D_HALF:8 * D_HALF + 2 * LORA_W]
    adn = z[..., 8 * D_HALF + 2 * LORA_W:]
    is_ctx = kv_ctx is None
    if is_ctx:
        oa = attn_dense(qa, ka, va)
        S0 = jnp.zeros((B, 2, N_HEADS_B, HEAD_DIM, HEAD_DIM), jnp.float32)
    else:
        oa = na_latent(qa, ka, va, kv_ctx[0], kv_ctx[1], rpb)
    ob, S = rwkv_branch(rkv, wdn, adn, w0, wup, a0, aup, k_k, k_a, r_k, gn_g, gn_b, S0)
    mixed = jnp.concatenate([oa.reshape(B, L, D_HALF) * jax.nn.silu(ga), ob * jax.nn.silu(gb)], axis=-1)
    out = mixed @ w_out
    if is_ctx:
        return out, (ka, va, S)
    return out


def hyena_filter_fft(L, fw1, fb1, fw2, fb2, fw3, fb3, freq, fwout):
    f32 = jnp.float32
    t = jnp.linspace(0.0, 1.0, L, dtype=f32)[:, None]
    ang = 2.0 * math.pi * jnp.arange(L, dtype=f32)[:, None] / L
    bands = jnp.linspace(1e-4, HY_BANDS - 1, HY_BANDS, dtype=f32)
    feats = jnp.concatenate([t, jnp.cos(ang * bands), -jnp.sin(ang * bands)], axis=-1)
    hdn = jnp.sin(freq[0] * (feats @ fw1 + fb1))
    hdn = jnp.sin(freq[1] * (hdn @ fw2 + fb2))
    hdn = jnp.sin(freq[2] * (hdn @ fw3 + fb3))
    filt = (hdn @ fwout).astype(f32)
    deltas = jnp.abs(jnp.linspace(math.log(HY_TARGET) / HY_SLOW, math.log(HY_TARGET) / HY_FAST, D_C, dtype=f32))
    window = jnp.exp(-t * deltas)
    h_f = filt[:, :D_C] * window
    h_b = filt[:, D_C:] * window
    two = jnp.concatenate([h_f, jnp.zeros((1, D_C), f32), h_b[:0:-1]], axis=0)
    two = two / jnp.sum(jnp.abs(two), axis=0, keepdims=True)
    return jnp.fft.rfft(two, axis=0)


def odd_mixer(h, w_in, w_out, conv_w, conv_b, fw1, fb1, fw2, fb2, fw3, fb3, freq, fwout, skip):
    B, L, _ = h.shape
    z = h @ w_in
    u = conv3(z[..., :3 * D_C], conv_w) + conv_b
    x0, x1, v = jnp.split(u, 3, axis=-1)
    g = z[..., 3 * D_C:]
    Hf = hyena_filter_fft(L, fw1, fb1, fw2, fb2, fw3, fb3, freq, fwout)
    uu = (x1 * v).astype(jnp.float32)
    y = jnp.fft.irfft(jnp.fft.rfft(uu, n=2 * L, axis=1) * Hf[None], n=2 * L, axis=1)[:, :L] + uu * skip
    y = x0 * y.astype(h.dtype)
    return (jax.nn.silu(g) * y) @ w_out


def setup_inputs(seed: int = 0) -> dict:
    key = jax.random.key(seed)
    ks = iter(jax.random.split(key, 40))

    def nrm(shape, scale):
        return jax.random.normal(next(ks), shape, jnp.float32) * scale

    d_in = D_MODEL ** -0.5
    return {
        'x_prompt': nrm((BATCH, SEQ, D_MODEL), 1.0),
        'x_sample': nrm((DEC_BATCH, DEC_SEQ, D_MODEL), 1.0),
        'c': nrm((DEC_BATCH, D_MODEL), 1.0),
        'cache_a_k': nrm((DEC_BATCH, N_EVEN, PAST_LEN, N_HEADS_A, HEAD_DIM), 1.0),
        'cache_a_v': nrm((DEC_BATCH, N_EVEN, PAST_LEN, N_HEADS_A, HEAD_DIM), 1.0),
        'state_b': nrm((DEC_BATCH, N_EVEN, 2, N_HEADS_B, HEAD_DIM, HEAD_DIM), 0.5),
        'c_ctx': nrm((D_MODEL,), 1.0),
        'w_ada': nrm((DEPTH, D_MODEL, 3 * D_MODEL), 0.5 * d_in),
        'b_ada': nrm((DEPTH, 3 * D_MODEL), 0.01),
        'ln_g': 1.0 + nrm((DEPTH, D_MODEL), 0.01),
        'ln_b': nrm((DEPTH, D_MODEL), 0.01),
        'w_in_even': nrm((N_EVEN, D_MODEL, E_COLS), d_in),
        'w_out_even': nrm((N_EVEN, D_MODEL, D_MODEL), BETA * d_in),
        'a_rpb': nrm((N_EVEN, N_HEADS_A, 2 * WIN_H - 1, 2 * WIN_W - 1), 0.02),
        'b_conv_w': nrm((N_EVEN, 3, 3 * D_HALF), 0.5),
        'b_w0': nrm((N_EVEN, 2, D_HALF), 0.1),
        'b_wup': nrm((N_EVEN, 2, LORA_W, D_HALF), 0.5 * LORA_W ** -0.5),
        'b_a0': nrm((N_EVEN, 2, D_HALF), 0.1),
        'b_aup': nrm((N_EVEN, 2, LORA_A, D_HALF), 0.5 * LORA_A ** -0.5),
        'b_kk': 1.0 + nrm((N_EVEN, D_HALF), 0.05),
        'b_ka': 1.0 + nrm((N_EVEN, D_HALF), 0.05),
        'b_rk': nrm((N_EVEN, N_HEADS_B, HEAD_DIM), 0.1),
        'b_gn_g': 1.0 + nrm((N_EVEN, D_HALF), 0.01),
        'b_gn_b': nrm((N_EVEN, D_HALF), 0.01),
        'w_in_odd': nrm((N_ODD, D_MODEL, 4 * D_C), d_in),
        'w_out_odd': nrm((N_ODD, D_C, D_MODEL), BETA * D_C ** -0.5),
        'h_conv_w': nrm((N_ODD, 3, 3 * D_C), 0.5),
        'h_conv_b': nrm((N_ODD, 3 * D_C), 0.01),
        'h_ffn_w1': nrm((N_ODD, HY_EMB, HY_ORDER), HY_EMB ** -0.5),
        'h_ffn_b1': nrm((N_ODD, HY_ORDER), 0.01),
        'h_ffn_w2': nrm((N_ODD, HY_ORDER, HY_ORDER), HY_ORDER ** -0.5),
        'h_ffn_b2': nrm((N_ODD, HY_ORDER), 0.01),
        'h_ffn_w3': nrm((N_ODD, HY_ORDER, HY_ORDER), HY_ORDER ** -0.5),
        'h_ffn_b3': nrm((N_ODD, HY_ORDER), 0.01),
        'h_freq': 1.0 + nrm((N_ODD, 3, HY_ORDER), 0.01),
        'h_ffn_wout': nrm((N_ODD, HY_ORDER, 2 * D_C), HY_ORDER ** -0.5),
        'h_skip': nrm((N_ODD, D_C), 1.0),
    }


def reference(x_prompt, x_sample, c, cache_a_k, cache_a_v, state_b, c_ctx, w_ada, b_ada, ln_g, ln_b,
              w_in_even, w_out_even, a_rpb, b_conv_w, b_w0, b_wup, b_a0, b_aup, b_kk, b_ka, b_rk,
              b_gn_g, b_gn_b, w_in_odd, w_out_odd, h_conv_w, h_conv_b, h_ffn_w1, h_ffn_b1, h_ffn_w2,
              h_ffn_b2, h_ffn_w3, h_ffn_b3, h_freq, h_ffn_wout, h_skip):
    x = x_prompt
    y = x_sample
    cond_ctx = c_ctx[None, :]
    new_k, new_v, new_s = [], [], []
    for l in range(DEPTH):
        sh_x, sc_x, gt_x = ada_mod(cond_ctx, w_ada[l], b_ada[l])
        sh_y, sc_y, gt_y = ada_mod(c, w_ada[l], b_ada[l])
        hx = x * (1.0 + sc_x) + sh_x
        hy = y * (1.0 + sc_y) + sh_y
        if l % 2 == 0:
            e = l // 2
            pe = (w_in_even[e], w_out_even[e], a_rpb[e], b_conv_w[e], b_w0[e], b_wup[e], b_a0[e],
                  b_aup[e], b_kk[e], b_ka[e], b_rk[e], b_gn_g[e], b_gn_b[e])
            ox, (kc, vc, sfin) = even_mixer(hx, *pe, None, None)
            new_k.append(kc)
            new_v.append(vc)
            new_s.append(sfin)
            oy = even_mixer(hy, *pe, (cache_a_k[:, e], cache_a_v[:, e]), state_b[:, e])
        else:
            o = l // 2
            po = (w_in_odd[o], w_out_odd[o], h_conv_w[o], h_conv_b[o], h_ffn_w1[o], h_ffn_b1[o],
                  h_ffn_w2[o], h_ffn_b2[o], h_ffn_w3[o], h_ffn_b3[o], h_freq[o], h_ffn_wout[o], h_skip[o])
            ox = odd_mixer(hx, *po)
            oy = odd_mixer(hy, *po)
        x = layer_norm(ALPHA * x + gt_x * ox, ln_g[l], ln_b[l])
        y = layer_norm(ALPHA * y + gt_y * oy, ln_g[l], ln_b[l])
    new_cache_a_k = jnp.stack(new_k, axis=1)
    new_cache_a_v = jnp.stack(new_v, axis=1)
    new_state_b = jnp.stack(new_s, axis=1)
    return (x, y, new_cache_a_k, new_cache_a_v, new_state_b)
```

```python
import functools
import math

import jax
import jax.numpy as jnp
from jax import lax
from jax.experimental import pallas as pl
from jax.experimental.pallas import tpu as pltpu

F32 = jnp.float32
BF16 = jnp.bfloat16

HEAD_DIM = 64
LORA = 64
GRID_W = 64
WIN_H = 8
WIN_W = 16
HY_TARGET = 1e-2
HY_FAST = 0.3
HY_SLOW = 1.5
LN_EPS = 1e-5
GN_EPS = 64e-5
NEG = -1e30

LANES = 128
SUBLANES = 8
VMEM_LIMIT = 56 * 1024 * 1024


def _tile(n, target, mult):
    if n <= target:
        return n
    t = (target // mult) * mult
    while t >= mult:
        if n % t == 0:
            return t
        t -= mult
    return n


def _params(*sem):
    return pltpu.CompilerParams(dimension_semantics=sem, vmem_limit_bytes=VMEM_LIMIT)


def _silu(x):
    return x * jax.nn.sigmoid(x)


def _dot(a, b):
    return jnp.dot(a, b, preferred_element_type=F32)


def _dot_nt(a, b):
    return lax.dot_general(a, b, (((1,), (1,)), ((), ())), preferred_element_type=F32)


def _head_ones():
    r = lax.broadcasted_iota(jnp.int32, (LANES, LANES), 0) // HEAD_DIM
    c = lax.broadcasted_iota(jnp.int32, (LANES, LANES), 1) // HEAD_DIM
    return (r == c).astype(BF16)


def _head_sum(x, ones):
    hi = x.astype(BF16)
    r1 = x - hi.astype(F32)
    mid = r1.astype(BF16)
    lo = (r1 - mid.astype(F32)).astype(BF16)
    return _dot(hi, ones) + _dot(mid, ones) + _dot(lo, ones)


def _ada_kernel(c_ref, w_ref, b_ref, o_ref):
    c = c_ref[...]
    o_ref[...] = _dot(_silu(c).astype(BF16), w_ref[...].astype(BF16)) + b_ref[...]


def _ada_mod(cond, w_ada, b_ada):
    depth, d, n = w_ada.shape
    r = cond.shape[0]
    tn = _tile(n, 768, LANES)
    return pl.pallas_call(
        _ada_kernel,
        grid=(depth, n // tn),
        in_specs=[pl.BlockSpec((r, d), lambda l, j: (0, 0)),
                  pl.BlockSpec((None, d, tn), lambda l, j: (l, 0, j)),
                  pl.BlockSpec((None, 1, tn), lambda l, j: (l, 0, j))],
        out_specs=pl.BlockSpec((None, r, tn), lambda l, j: (l, 0, j)),
        out_shape=jax.ShapeDtypeStruct((depth, r, n), F32),
        compiler_params=_params("parallel", "parallel"),
        name="ada_mod",
    )(cond, w_ada, b_ada.reshape(depth, 1, n))


def _modmm_kernel(x_ref, sc_ref, sh_ref, w_ref, o_ref, h_ref):
    @pl.when(pl.program_id(2) == 0)
    def _():
        h_ref[...] = (x_ref[...] * (1.0 + sc_ref[...]) + sh_ref[...]).astype(BF16)

    o_ref[...] = _dot(h_ref[...], w_ref[...])


def _modmm(x, sc, sh, w):
    b, l, d = x.shape
    n = w.shape[1]
    tl = _tile(l, 1024, SUBLANES)
    tn = _tile(n, 1024, LANES)
    return pl.pallas_call(
        _modmm_kernel,
        grid=(b, l // tl, n // tn),
        in_specs=[pl.BlockSpec((None, tl, d), lambda bb, i, j: (bb, i, 0)),
                  pl.BlockSpec((None, 1, d), lambda bb, i, j: (bb, 0, 0)),
                  pl.BlockSpec((None, 1, d), lambda bb, i, j: (bb, 0, 0)),
                  pl.BlockSpec((d, tn), lambda bb, i, j: (0, j))],
        out_specs=pl.BlockSpec((None, tl, tn), lambda bb, i, j: (bb, i, j)),
        out_shape=jax.ShapeDtypeStruct((b, l, n), F32),
        scratch_shapes=[pltpu.VMEM((tl, d), BF16)],
        compiler_params=_params("parallel", "parallel", "arbitrary"),
        name="mod_in_proj",
    )(x, sc, sh, w)


def _outln_kernel(*refs, nparts, alpha):
    m_refs = refs[:nparts]
    w_refs = refs[nparts:2 * nparts]
    x_ref, gt_ref, g_ref, b_ref, o_ref = refs[2 * nparts:]
    acc = _dot(m_refs[0][...], w_refs[0][...])
    for p in range(1, nparts):
        acc = acc + _dot(m_refs[p][...], w_refs[p][...])
    y = alpha * x_ref[...] + gt_ref[...] * acc
    mu = jnp.mean(y, axis=-1, keepdims=True)
    yc = y - mu
    var = jnp.mean(yc * yc, axis=-1, keepdims=True)
    o_ref[...] = yc * lax.rsqrt(var + LN_EPS) * g_ref[...] + b_ref[...]


def _outln(parts, ws, x, gt, ln_g, ln_b, alpha):
    b, l, d = x.shape
    tl = _tile(l, 512, SUBLANES)
    nparts = len(parts)
    in_specs = [pl.BlockSpec((None, tl, p.shape[2]), lambda bb, i: (bb, i, 0)) for p in parts]
    in_specs += [pl.BlockSpec(w.shape, lambda bb, i: (0, 0)) for w in ws]
    in_specs += [pl.BlockSpec((None, tl, d), lambda bb, i: (bb, i, 0)),
                 pl.BlockSpec((None, 1, d), lambda bb, i: (bb, 0, 0)),
                 pl.BlockSpec((1, d), lambda bb, i: (0, 0)),
                 pl.BlockSpec((1, d), lambda bb, i: (0, 0))]
    return pl.pallas_call(
        functools.partial(_outln_kernel, nparts=nparts, alpha=alpha),
        grid=(b, l // tl),
        in_specs=in_specs,
        out_specs=pl.BlockSpec((None, tl, d), lambda bb, i: (bb, i, 0)),
        out_shape=jax.ShapeDtypeStruct((b, l, d), F32),
        compiler_params=_params("parallel", "parallel"),
        name="out_proj_ln",
    )(*parts, *ws, x, gt, ln_g.reshape(1, d), ln_b.reshape(1, d))


def _conv_specs(tl, tc, l, colblk):
    g = tl // SUBLANES
    last = l // SUBLANES - 1
    return [
        pl.BlockSpec((None, tl, tc), lambda bb, i, j: (bb, i, colblk + j)),
        pl.BlockSpec((None, SUBLANES, tc), lambda bb, i, j: (bb, jnp.maximum(i * g - 1, 0), colblk + j)),
        pl.BlockSpec((None, SUBLANES, tc), lambda bb, i, j: (bb, jnp.minimum((i + 1) * g, last), colblk + j)),
    ]


def _conv3(x_ref, p_ref, n_ref, w, first, last):
    x = x_ref[...]
    tl = x.shape[0]
    row = lax.broadcasted_iota(jnp.int32, x.shape, 0)
    prow = jnp.where(first, 0.0, p_ref[SUBLANES - 1:SUBLANES, :])
    nrow = jnp.where(last, 0.0, n_ref[0:1, :])
    xm = jnp.where(row == 0, prow, pltpu.roll(x, 1, 0))
    xq = jnp.where(row == tl - 1, nrow, pltpu.roll(x, tl - 1, 0))
    return xm * w[0:1, :] + x * w[1:2, :] + xq * w[2:3, :]


def _hy_conv_kernel(x0_ref, x0p_ref, x0n_ref, x1_ref, x1p_ref, x1n_ref, v_ref, vp_ref, vn_ref,
                    w0_ref, w1_ref, w2_ref, b0_ref, b1_ref, b2_ref, x0o_ref, uu_ref, uub_ref):
    first = pl.program_id(1) == 0
    last = pl.program_id(1) == pl.num_programs(1) - 1
    x0 = _conv3(x0_ref, x0p_ref, x0n_ref, w0_ref[...], first, last) + b0_ref[...]
    x1 = _conv3(x1_ref, x1p_ref, x1n_ref, w1_ref[...], first, last) + b1_ref[...]
    v = _conv3(v_ref, vp_ref, vn_ref, w2_ref[...], first, last) + b2_ref[...]
    uu = x1 * v
    x0o_ref[...] = x0
    uu_ref[...] = uu
    uub_ref[...] = uu.astype(BF16)


def _hy_conv(z, conv_w, conv_b, c):
    b, l, _ = z.shape
    tl = _tile(l, 512, SUBLANES)
    tc = _tile(c, 512, LANES)
    nc = c // tc
    in_specs = _conv_specs(tl, tc, l, 0) + _conv_specs(tl, tc, l, nc) + _conv_specs(tl, tc, l, 2 * nc)
    in_specs += [pl.BlockSpec((3, tc), lambda bb, i, j, k=k: (0, k * nc + j)) for k in range(3)]
    in_specs += [pl.BlockSpec((1, tc), lambda bb, i, j, k=k: (0, k * nc + j)) for k in range(3)]
    ospec = pl.BlockSpec((None, tl, tc), lambda bb, i, j: (bb, i, j))
    cb = conv_b.reshape(1, 3 * c)
    return pl.pallas_call(
        _hy_conv_kernel,
        grid=(b, l // tl, nc),
        in_specs=in_specs,
        out_specs=[ospec, ospec, ospec],
        out_shape=[jax.ShapeDtypeStruct((b, l, c), F32), jax.ShapeDtypeStruct((b, l, c), F32),
                   jax.ShapeDtypeStruct((b, l, c), BF16)],
        compiler_params=_params("parallel", "parallel", "parallel"),
        name="hyena_gate_conv",
    )(z, z, z, z, z, z, z, z, z, conv_w, conv_w, conv_w, cb, cb, cb)


def _hy_filter_kernel(feats_ref, t_ref, fw1_ref, fb1_ref, fw2_ref, fb2_ref, fw3_ref, fb3_ref, freq_ref,
                      wf_ref, wb_ref, dl_ref, o_ref):
    freq = freq_ref[...]
    hdn = jnp.sin(freq[0:1, :] * (_dot(feats_ref[...].astype(BF16), fw1_ref[...].astype(BF16)) + fb1_ref[...]))
    hdn = jnp.sin(freq[1:2, :] * (_dot(hdn.astype(BF16), fw2_ref[...].astype(BF16)) + fb2_ref[...]))
    hdn = jnp.sin(freq[2:3, :] * (_dot(hdn.astype(BF16), fw3_ref[...].astype(BF16)) + fb3_ref[...]))
    hb16 = hdn.astype(BF16)
    window = jnp.exp(-t_ref[...] * dl_ref[...])
    h_f = _dot(hb16, wf_ref[...].astype(BF16)) * window
    h_b = _dot(hb16, wb_ref[...].astype(BF16)) * window
    row = lax.broadcasted_iota(jnp.int32, h_b.shape, 0)
    h_b = jnp.where(row == 0, 0.0, h_b)
    norm = jnp.sum(jnp.abs(h_f), axis=0, keepdims=True) + jnp.sum(jnp.abs(h_b), axis=0, keepdims=True)
    inv = 1.0 / norm
    o_ref[0] = ((h_f + h_b) * inv).astype(BF16)
    o_ref[1] = ((h_f - h_b) * inv).astype(BF16)


def _hy_filter(l, c, fw1, fb1, fw2, fb2, fw3, fb3, freq, fwout):
    emb, order = fw1.shape
    bands_n = (emb - 1) // 2
    t = jnp.linspace(0.0, 1.0, l, dtype=F32)[:, None]
    ang = 2.0 * math.pi * jnp.arange(l, dtype=F32)[:, None] / l
    bands = jnp.linspace(1e-4, bands_n - 1, bands_n, dtype=F32)
    feats = jnp.concatenate([t, jnp.cos(ang * bands), -jnp.sin(ang * bands)], axis=-1)
    feats = jnp.pad(feats, ((0, 0), (0, LANES - emb)))
    fw1p = jnp.pad(fw1, ((0, LANES - emb), (0, 0)))
    deltas = jnp.abs(jnp.linspace(math.log(HY_TARGET) / HY_SLOW, math.log(HY_TARGET) / HY_FAST, c, dtype=F32))
    tc = _tile(c, 256, LANES)
    nc = c // tc
    full = lambda shape: pl.BlockSpec(shape, lambda j: tuple(0 for _ in shape))
    return pl.pallas_call(
        _hy_filter_kernel,
        grid=(nc,),
        in_specs=[full((l, LANES)), full((l, 1)), full((LANES, order)), full((1, order)),
                  full((order, order)), full((1, order)), full((order, order)), full((1, order)),
                  full((3, order)),
                  pl.BlockSpec((order, tc), lambda j: (0, j)),
                  pl.BlockSpec((order, tc), lambda j: (0, nc + j)),
                  pl.BlockSpec((1, tc), lambda j: (0, j))],
        out_specs=pl.BlockSpec((2, l, tc), lambda j: (0, 0, j)),
        out_shape=jax.ShapeDtypeStruct((2, l, c), BF16),
        compiler_params=_params("parallel"),
        name="hyena_filter",
    )(feats, t, fw1p, fb1.reshape(1, order), fw2, fb2.reshape(1, order), fw3, fb3.reshape(1, order),
      freq, fwout, fwout, deltas.reshape(1, c))


def _dft_tile(l):
    return _tile(l, 512, SUBLANES)


def _dft_mats(l):
    th = _dft_tile(l)
    nt = l // th
    k = jnp.arange(l, dtype=jnp.int32)
    m = (k[:, None] * k[None, :]) % (2 * l)
    ang = m.astype(F32) * (math.pi / l)
    cos = jnp.cos(ang)
    sin = -jnp.sin(ang)
    nyq = jnp.where(k % 2 == 0, 1.0, -1.0).astype(F32)
    sin = jnp.where((k == 0)[:, None], nyq[None, :], sin)
    fwd = jnp.stack([cos.reshape(nt, th, l), sin.reshape(nt, th, l)], axis=1).reshape(2 * l, l)
    scale = jnp.where(k == 0, 0.5 / l, 1.0 / l).astype(F32)
    scale = jnp.stack([scale.reshape(nt, th), scale.reshape(nt, th)], axis=1).reshape(2 * l)
    inv = (fwd * scale[:, None]).T
    return fwd.astype(BF16), inv.astype(BF16)


def _dft_plain_kernel(f_ref, x_ref, o_ref):
    o_ref[...] = _dot(f_ref[...], x_ref[...])


def _dft_plain(fwd, x):
    b, l, c = x.shape
    tm = 2 * _dft_tile(l)
    tn = _tile(c, 512, LANES)
    return pl.pallas_call(
        _dft_plain_kernel,
        grid=(b, c // tn, 2 * l // tm),
        in_specs=[pl.BlockSpec((tm, l), lambda bb, j, m: (m, 0)),
                  pl.BlockSpec((None, l, tn), lambda bb, j, m: (bb, 0, j))],
        out_specs=pl.BlockSpec((None, tm, tn), lambda bb, j, m: (bb, m, j)),
        out_shape=jax.ShapeDtypeStruct((b, 2 * l, c), F32),
        compiler_params=_params("parallel", "parallel", "parallel"),
        name="hyena_filter_dft",
    )(fwd, x)


def _dft_fwd_kernel(f_ref, x_ref, hre_ref, him_ref, o_ref, *, th):
    acc = _dot(f_ref[...], x_ref[...])
    re = acc[:th]
    im = acc[th:]
    hre = hre_ref[...]
    him = him_ref[...]
    row = lax.broadcasted_iota(jnp.int32, re.shape, 0)
    dc = jnp.logical_and(pl.program_id(2) == 0, row == 0)
    yre = jnp.where(dc, re * hre, re * hre - im * him)
    yim = jnp.where(dc, im * him, re * him + im * hre)
    o_ref[:th] = yre.astype(BF16)
    o_ref[th:] = yim.astype(BF16)


def _dft_fwd(fwd, x, hre, him):
    b, l, c = x.shape
    th = _dft_tile(l)
    tn = _tile(c, 512, LANES)
    return pl.pallas_call(
        functools.partial(_dft_fwd_kernel, th=th),
        grid=(b, c // tn, l // th),
        in_specs=[pl.BlockSpec((2 * th, l), lambda bb, j, m: (m, 0)),
                  pl.BlockSpec((None, l, tn), lambda bb, j, m: (bb, 0, j)),
                  pl.BlockSpec((th, tn), lambda bb, j, m: (m, j)),
                  pl.BlockSpec((th, tn), lambda bb, j, m: (m, j))],
        out_specs=pl.BlockSpec((None, 2 * th, tn), lambda bb, j, m: (bb, m, j)),
        out_shape=jax.ShapeDtypeStruct((b, 2 * l, c), BF16),
        compiler_params=_params("parallel", "parallel", "parallel"),
        name="hyena_dft_fwd",
    )(fwd, x, hre, him)


def _dft_inv_kernel(g_ref, y_ref, uu_ref, x0_ref, gz_ref, skip_ref, o_ref):
    y = _dot(g_ref[...], y_ref[...]) + uu_ref[...] * skip_ref[...]
    o_ref[...] = (_silu(gz_ref[...]) * (x0_ref[...] * y)).astype(BF16)


def _dft_inv(inv, yf, uu, x0, z, skip):
    b, l, c = uu.shape
    tm = _tile(l, 512, SUBLANES)
    tn = _tile(c, 256, LANES)
    nc = c // tn
    tile = pl.BlockSpec((None, tm, tn), lambda bb, j, i: (bb, i, j))
    return pl.pallas_call(
        _dft_inv_kernel,
        grid=(b, nc, l // tm),
        in_specs=[pl.BlockSpec((tm, 2 * l), lambda bb, j, i: (i, 0)),
                  pl.BlockSpec((None, 2 * l, tn), lambda bb, j, i: (bb, 0, j)),
                  tile, tile,
                  pl.BlockSpec((None, tm, tn), lambda bb, j, i: (bb, i, 3 * nc + j)),
                  pl.BlockSpec((1, tn), lambda bb, j, i: (0, j))],
        out_specs=tile,
        out_shape=jax.ShapeDtypeStruct((b, l, c), BF16),
        compiler_params=_params("parallel", "parallel", "parallel"),
        name="hyena_dft_inv",
    )(inv, yf, uu, x0, z, skip.reshape(1, c))


def _hy_spectrum(l, c, fwd, fw1, fb1, fw2, fb2, fw3, fb3, freq, fwout):
    th = _dft_tile(l)
    nt = l // th
    hpm = _hy_filter(l, c, fw1, fb1, fw2, fb2, fw3, fb3, freq, fwout)
    spec = _dft_plain(fwd, hpm).reshape(2, nt, 2, th, c)
    hre = spec[0, :, 0].reshape(l, c)
    him = spec[1, :, 1].reshape(l, c)
    him = him.at[0].set(spec[0, 0, 1, 0])
    return hre, him


def _lane_mask(shape):
    return lax.broadcasted_iota(jnp.int32, shape, 1) < HEAD_DIM


def _ctx_attn_kernel(q_ref, k_ref, v_ref, ga_ref, o_ref):
    q = q_ref[...]
    k = k_ref[...].astype(BF16)
    v = v_ref[...].astype(BF16)
    m0 = _lane_mask(q.shape)
    scale = HEAD_DIM ** -0.5
    outs = []
    for h in range(2):
        qh = jnp.where(m0 if h == 0 else jnp.logical_not(m0), q, 0.0).astype(BF16)
        s = _dot_nt(qh, k) * scale
        p = jnp.exp(s - jnp.max(s, axis=-1, keepdims=True))
        den = jnp.sum(p, axis=-1, keepdims=True)
        outs.append(_dot(p.astype(BF16), v) / den)
    o = jnp.where(m0, outs[0], outs[1])
    o_ref[...] = (o * _silu(ga_ref[...])).astype(BF16)


def _ctx_attn(z, dh):
    b, l, _ = z.shape
    nb = dh // LANES
    spec = lambda g: pl.BlockSpec((None, l, LANES), lambda bb, hp, g=g: (bb, 0, g * nb + hp))
    return pl.pallas_call(
        _ctx_attn_kernel,
        grid=(b, nb),
        in_specs=[spec(0), spec(1), spec(2), spec(3)],
        out_specs=pl.BlockSpec((None, l, LANES), lambda bb, hp: (bb, 0, hp)),
        out_shape=jax.ShapeDtypeStruct((b, l, dh), BF16),
        compiler_params=_params("parallel", "parallel"),
        name="ctx_attention",
    )(z, z, z, z)


def _na_bias_table(rpb):
    c = jnp.arange(GRID_W)[:, None]
    kc = jnp.arange(GRID_W)[None, :]
    qstart = jnp.clip(c - WIN_W // 2, 0, GRID_W - WIN_W)
    mask = (kc >= qstart) & (kc < qstart + WIN_W)
    cidx = jnp.clip(kc - c, 1 - WIN_W, WIN_W - 1) + WIN_W - 1
    dr = jnp.arange(WIN_H)[:, None] + jnp.arange(WIN_H)[None, :]
    t = rpb[:, dr][..., cidx]
    t = jnp.where(mask, t, NEG)
    return t.transpose(0, 1, 3, 2, 4).reshape(rpb.shape[0], WIN_H, GRID_W, WIN_H * GRID_W)


def _na_kernel(q_ref, k_ref, v_ref, kc_ref, vc_ref, bias_ref, ga_ref, o_ref, *, rows):
    kc = kc_ref[...].astype(BF16)
    vc = vc_ref[...].astype(BF16)
    m0 = _lane_mask((GRID_W, LANES))
    scale = HEAD_DIM ** -0.5
    win = WIN_H * GRID_W

    def body(r, carry):
        rs = jnp.clip(r - WIN_H // 2, 0, rows - WIN_H)
        d0 = rs - r + WIN_H - 1
        q0 = pl.multiple_of(r * GRID_W, GRID_W)
        k0 = pl.multiple_of(rs * GRID_W, GRID_W)
        q = q_ref[pl.ds(q0, GRID_W), :]
        kw = k_ref[pl.ds(k0, win), :].astype(BF16)
        vw = v_ref[pl.ds(k0, win), :].astype(BF16)
        outs = []
        for h in range(2):
            qh = jnp.where(m0 if h == 0 else jnp.logical_not(m0), q, 0.0).astype(BF16)
            s_nb = _dot_nt(qh, kw) * scale + bias_ref[h, d0]
            s_cx = _dot_nt(qh, kc) * scale
            m = jnp.maximum(jnp.max(s_nb, axis=-1, keepdims=True), jnp.max(s_cx, axis=-1, keepdims=True))
            p_nb = jnp.exp(s_nb - m)
            p_cx = jnp.exp(s_cx - m)
            den = jnp.sum(p_nb, axis=-1, keepdims=True) + jnp.sum(p_cx, axis=-1, keepdims=True)
            outs.append((_dot(p_nb.astype(BF16), vw) + _dot(p_cx.astype(BF16), vc)) / den)
        o = jnp.where(m0, outs[0], outs[1])
        ga = ga_ref[pl.ds(q0, GRID_W), :]
        o_ref[pl.ds(q0, GRID_W), :] = (o * _silu(ga)).astype(BF16)
        return carry

    lax.fori_loop(0, rows, body, 0)


def _na_attn(z, dh, k_ctx, v_ctx, bias):
    b, t, _ = z.shape
    rows = t // GRID_W
    assert t % GRID_W == 0 and rows >= WIN_H
    p = k_ctx.shape[1]
    nb = dh // LANES
    spec = lambda g: pl.BlockSpec((None, t, LANES), lambda bb, hp, g=g: (bb, 0, g * nb + hp))
    cspec = pl.BlockSpec((None, p, LANES), lambda bb, hp: (bb, 0, hp))
    return pl.pallas_call(
        functools.partial(_na_kernel, rows=rows),
        grid=(b, nb),
        in_specs=[spec(0), spec(1), spec(2), cspec, cspec,
                  pl.BlockSpec((2, WIN_H, GRID_W, WIN_H * GRID_W), lambda bb, hp: (hp, 0, 0, 0)),
                  spec(3)],
        out_specs=pl.BlockSpec((None, t, LANES), lambda bb, hp: (bb, 0, hp)),
        out_shape=jax.ShapeDtypeStruct((b, t, dh), BF16),
        compiler_params=_params("parallel", "parallel"),
        name="na_attention",
    )(z, z, z, k_ctx, v_ctx, bias, z)


N_PREP = 10


def _softplus(x):
    return jnp.maximum(x, 0.0) + jnp.log(1.0 + jnp.exp(-jnp.abs(x)))


def _rwkv_prep_kernel(r_ref, rp_ref, rn_ref, k_ref, kp_ref, kn_ref, v_ref, vp_ref, vn_ref,
                      cwr_ref, cwk_ref, cwv_ref, wdn_ref, adn_ref, wup_ref, aup_ref, w0_ref, a0_ref,
                      kk_ref, ka_ref, rk_ref, o_ref):
    first = pl.program_id(1) == 0
    last = pl.program_id(1) == pl.num_programs(1) - 1
    r = _conv3(r_ref, rp_ref, rn_ref, cwr_ref[...], first, last)
    k = _conv3(k_ref, kp_ref, kn_ref, cwk_ref[...], first, last)
    v = _conv3(v_ref, vp_ref, vn_ref, cwv_ref[...], first, last)
    ones = _head_ones()
    tw = jnp.tanh(wdn_ref[...]).astype(BF16)
    ad = adn_ref[...].astype(BF16)
    k_a = ka_ref[...]
    kk = k * kk_ref[...]
    nrm = jnp.sqrt(_head_sum(kk * kk, ones))
    kk = kk / jnp.maximum(nrm, 1e-12)
    o_ref[0] = r
    o_ref[1] = v
    o_ref[2] = kk
    kd_sum = None
    for d in range(2):
        w_raw = -_softplus(-(w0_ref[d:d + 1, :] + _dot(tw, wup_ref[d].astype(BF16)))) - 0.5
        decay = jnp.exp(-jnp.exp(w_raw))
        a = jax.nn.sigmoid(a0_ref[d:d + 1, :] + _dot(ad, aup_ref[d].astype(BF16)))
        kd = k * (1.0 + (a - 1.0) * k_a)
        o_ref[3 + d] = decay
        o_ref[5 + d] = kd
        o_ref[7 + d] = kk * a
        kd_sum = kd if kd_sum is None else kd_sum + kd
    o_ref[9] = _head_sum(r * kd_sum * rk_ref[...], ones) * v


def _rwkv_prep(z, dh, conv_w, w0, wup, a0, aup, k_k, k_a, r_k):
    b, l, _ = z.shape
    tl = _tile(l, 512, SUBLANES)
    nb = dh // LANES
    base = 4 * nb
    lora_blk = 8 * nb
    in_specs = (_conv_specs(tl, LANES, l, base) + _conv_specs(tl, LANES, l, base + nb)
                + _conv_specs(tl, LANES, l, base + 2 * nb))
    in_specs += [pl.BlockSpec((3, LANES), lambda bb, i, j, g=g: (0, g * nb + j)) for g in range(3)]
    in_specs += [pl.BlockSpec((None, tl, LANES), lambda bb, i, j: (bb, i, lora_blk)),
                 pl.BlockSpec((None, tl, LANES), lambda bb, i, j: (bb, i, lora_blk + 1)),
                 pl.BlockSpec((2, LANES, LANES), lambda bb, i, j: (0, 0, j)),
                 pl.BlockSpec((2, LANES, LANES), lambda bb, i, j: (0, 0, j)),
                 pl.BlockSpec((2, LANES), lambda bb, i, j: (0, j)),
                 pl.BlockSpec((2, LANES), lambda bb, i, j: (0, j)),
                 pl.BlockSpec((1, LANES), lambda bb, i, j: (0, j)),
                 pl.BlockSpec((1, LANES), lambda bb, i, j: (0, j)),
                 pl.BlockSpec((1, LANES), lambda bb, i, j: (0, j))]
    zeros = jnp.zeros((LORA, dh), F32)
    wup_p = jnp.stack([jnp.concatenate([wup[0], zeros]), jnp.concatenate([zeros, wup[1]])])
    aup_p = jnp.stack([jnp.concatenate([aup[0], zeros]), jnp.concatenate([zeros, aup[1]])])
    return pl.pallas_call(
        _rwkv_prep_kernel,
        grid=(b, l // tl, nb),
        in_specs=in_specs,
        out_specs=pl.BlockSpec((N_PREP, None, tl, LANES), lambda bb, i, j: (0, bb, i, j)),
        out_shape=jax.ShapeDtypeStruct((N_PREP, b, l, dh), F32),
        compiler_params=_params("parallel", "parallel", "parallel"),
        name="rwkv_prep",
    )(z, z, z, z, z, z, z, z, z, conv_w, conv_w, conv_w, z, z, wup_p, aup_p, w0, a0,
      k_k.reshape(1, dh), k_a.reshape(1, dh), r_k.reshape(1, dh))


def _scan_kernel(r_ref, v_ref, kk_ref, w_ref, k_ref, b_ref, s0_ref, o_ref, sout_ref, s_ref, *, steps, nblk):
    d = pl.program_id(1)
    i = pl.program_id(2)

    @pl.when(i == 0)
    def _():
        s_ref[...] = s0_ref[...]

    def step(s, carry):
        tt = s + d * (steps - 1 - 2 * s)
        kk = kk_ref[tt]
        w = w_ref[tt]
        kd = k_ref[tt]
        kb = b_ref[tt]
        r = r_ref[tt]
        v = v_ref[tt]
        for n in range(HEAD_DIM):
            sn = s_ref[n]
            sa = -jnp.sum(sn * kk, axis=0, keepdims=True)
            sn = sn * w + sa * kb + v[n:n + 1, :] * kd
            s_ref[n] = sn
            o_ref[tt, n:n + 1, :] = jnp.sum(sn * r, axis=0, keepdims=True)
        return carry

    lax.fori_loop(0, steps, step, 0)

    @pl.when(i == nblk - 1)
    def _():
        sout_ref[...] = s_ref[...]


def _rwkv_scan(ops, s0):
    _, l, n, lanes = ops.shape
    steps = _tile(l, 32, 1)
    nblk = l // steps
    tb = lambda d, i: i + d * (nblk - 1 - 2 * i)
    shared = lambda idx: pl.BlockSpec((None, steps, n, LANES), lambda g, d, i: (idx, tb(d, i), 0, g))
    perdir = lambda idx: pl.BlockSpec((None, steps, n, LANES), lambda g, d, i: (idx + d, tb(d, i), 0, g))
    sspec = pl.BlockSpec((None, n, n, LANES), lambda g, d, i: (d, 0, 0, g))
    return pl.pallas_call(
        functools.partial(_scan_kernel, steps=steps, nblk=nblk),
        grid=(lanes // LANES, 2, nblk),
        in_specs=[shared(0), shared(1), shared(2), perdir(3), perdir(5), perdir(7), sspec],
        out_specs=[pl.BlockSpec((None, steps, n, LANES), lambda g, d, i: (d, tb(d, i), 0, g)), sspec],
        out_shape=[jax.ShapeDtypeStruct((2, l, n, lanes), F32), jax.ShapeDtypeStruct((2, n, n, lanes), F32)],
        scratch_shapes=[pltpu.VMEM((n, n, LANES), F32)],
        compiler_params=_params("parallel", "arbitrary", "arbitrary"),
        name="rwkv_scan",
    )(ops, ops, ops, ops, ops, ops, s0)


def _rwkv_post_kernel(of_ref, ob_ref, bonus_ref, gb_ref, g_ref, b_ref, o_ref):
    ones = _head_ones()
    o = of_ref[...] + ob_ref[...]
    mu = _head_sum(o, ones) * (1.0 / HEAD_DIM)
    oc = o - mu
    var = _head_sum(oc * oc, ones) * (1.0 / HEAD_DIM)
    o = oc * lax.rsqrt(var + GN_EPS) * g_ref[...] + b_ref[...] + bonus_ref[...]
    o_ref[...] = (o * _silu(gb_ref[...])).astype(BF16)


def _rwkv_post(o2, prep, z, dh, gn_g, gn_b):
    _, b, l, _ = o2.shape
    tl = _tile(l, 512, SUBLANES)
    nb = dh // LANES
    gate_blk = 7 * nb
    return pl.pallas_call(
        _rwkv_post_kernel,
        grid=(b, l // tl, nb),
        in_specs=[pl.BlockSpec((None, None, tl, LANES), lambda bb, i, j: (0, bb, i, j)),
                  pl.BlockSpec((None, None, tl, LANES), lambda bb, i, j: (1, bb, i, j)),
                  pl.BlockSpec((None, None, tl, LANES), lambda bb, i, j: (N_PREP - 1, bb, i, j)),
                  pl.BlockSpec((None, tl, LANES), lambda bb, i, j: (bb, i, gate_blk + j)),
                  pl.BlockSpec((1, LANES), lambda bb, i, j: (0, j)),
                  pl.BlockSpec((1, LANES), lambda bb, i, j: (0, j))],
        out_specs=pl.BlockSpec((None, tl, LANES), lambda bb, i, j: (bb, i, j)),
        out_shape=jax.ShapeDtypeStruct((b, l, dh), BF16),
        compiler_params=_params("parallel", "parallel", "parallel"),
        name="rwkv_post",
    )(o2, o2, prep, z, gn_g.reshape(1, dh), gn_b.reshape(1, dh))


def _rwkv_heads(z, dh, pe, s0):
    b, l, _ = z.shape
    nh = dh // HEAD_DIM
    prep = _rwkv_prep(z, dh, pe["conv_w"], pe["w0"], pe["wup"], pe["a0"], pe["aup"], pe["k_k"], pe["k_a"], pe["r_k"])
    seqs = b * nh
    lanes = -(-seqs // LANES) * LANES
    ops = prep[:N_PREP - 1].reshape(N_PREP - 1, b, l, nh, HEAD_DIM).transpose(0, 2, 4, 1, 3)
    ops = ops.reshape(N_PREP - 1, l, HEAD_DIM, seqs)
    if s0 is None:
        st = jnp.zeros((2, HEAD_DIM, HEAD_DIM, lanes), F32)
    else:
        st = s0.astype(F32).transpose(1, 3, 4, 0, 2).reshape(2, HEAD_DIM, HEAD_DIM, seqs)
    if lanes != seqs:
        ops = jnp.pad(ops, ((0, 0), (0, 0), (0, 0), (0, lanes - seqs)))
        if s0 is not None:
            st = jnp.pad(st, ((0, 0), (0, 0), (0, 0), (0, lanes - seqs)))
    o, s_fin = _rwkv_scan(ops, st)
    o2 = o[..., :seqs].reshape(2, l, HEAD_DIM, b, nh).transpose(0, 3, 1, 4, 2).reshape(2, b, l, dh)
    s_fin = s_fin[..., :seqs].reshape(2, HEAD_DIM, HEAD_DIM, b, nh).transpose(3, 0, 4, 1, 2)
    mixed = _rwkv_post(o2, prep, z, dh, pe["gn_g"], pe["gn_b"])
    return mixed, s_fin


def _even_layer(x, sc, sh, gt, ln_g, ln_b, alpha, pe, kv_ctx, s0):
    dh = pe["dh"]
    z = _modmm(x, sc, sh, pe["w_in"])
    if kv_ctx is None:
        mixed_a = _ctx_attn(z, dh)
    else:
        mixed_a = _na_attn(z, dh, kv_ctx[0], kv_ctx[1], pe["bias"])
    mixed_b, s_fin = _rwkv_heads(z, dh, pe, s0)
    out = _outln([mixed_a, mixed_b], [pe["w_out"][:dh], pe["w_out"][dh:]], x, gt, ln_g, ln_b, alpha)
    return out, z, s_fin


def _odd_layer(x, sc, sh, gt, ln_g, ln_b, alpha, po, spec, mats):
    c = po["c"]
    z = _modmm(x, sc, sh, po["w_in"])
    x0, uu, uub = _hy_conv(z, po["conv_w"], po["conv_b"], c)
    yf = _dft_fwd(mats[0], uub, spec[0], spec[1])
    mixed = _dft_inv(mats[1], yf, uu, x0, z, po["skip"])
    return _outln([mixed], [po["w_out"]], x, gt, ln_g, ln_b, alpha)


def kernel(x_prompt, x_sample, c, cache_a_k, cache_a_v, state_b, c_ctx, w_ada, b_ada, ln_g, ln_b,
           w_in_even, w_out_even, a_rpb, b_conv_w, b_w0, b_wup, b_a0, b_aup, b_kk, b_ka, b_rk,
           b_gn_g, b_gn_b, w_in_odd, w_out_odd, h_conv_w, h_conv_b, h_ffn_w1, h_ffn_b1, h_ffn_w2,
           h_ffn_b2, h_ffn_w3, h_ffn_b3, h_freq, h_ffn_wout, h_skip):
    depth, d, _ = w_ada.shape
    bx, lx, _ = x_prompt.shape
    by, ly, _ = x_sample.shape
    alpha = (2 * depth) ** 0.25
    dh = b_kk.shape[1]
    nh = dh // HEAD_DIM
    p_len = cache_a_k.shape[2]
    c_hy = h_skip.shape[1]

    rows = 1 + by
    rpad = -(-rows // SUBLANES) * SUBLANES
    cond = jnp.pad(jnp.concatenate([c_ctx[None, :], c], axis=0), ((0, rpad - rows), (0, 0)))
    mods = _ada_mod(cond, w_ada, b_ada)

    x = x_prompt
    y = x_sample
    mats_x = _dft_mats(lx) if depth > 1 else None
    mats_y = _dft_mats(ly) if depth > 1 else None
    new_k, new_v, new_s = [], [], []
    for l in range(depth):
        m = mods[l]
        sh_x, sc_x, gt_x = (jnp.broadcast_to(m[0:1, k * d:(k + 1) * d][None], (bx, 1, d)) for k in range(3))
        sh_y, sc_y, gt_y = (m[1:rows, k * d:(k + 1) * d][:, None, :] for k in range(3))
        if l % 2 == 0:
            e = l // 2
            pe = dict(dh=dh, w_in=w_in_even[e].astype(BF16), w_out=w_out_even[e].astype(BF16),
                      bias=_na_bias_table(a_rpb[e]), conv_w=b_conv_w[e], w0=b_w0[e], wup=b_wup[e], a0=b_a0[e],
                      aup=b_aup[e], k_k=b_kk[e], k_a=b_ka[e], r_k=b_rk[e], gn_g=b_gn_g[e], gn_b=b_gn_b[e])
            x, zx, s_fin = _even_layer(x, sc_x, sh_x, gt_x, ln_g[l], ln_b[l], alpha, pe, None, None)
            new_k.append(zx[..., dh:2 * dh].reshape(bx, lx, nh, HEAD_DIM))
            new_v.append(zx[..., 2 * dh:3 * dh].reshape(bx, lx, nh, HEAD_DIM))
            new_s.append(s_fin)
            kv = (cache_a_k[:, e].reshape(by, p_len, dh), cache_a_v[:, e].reshape(by, p_len, dh))
            y, _, _ = _even_layer(y, sc_y, sh_y, gt_y, ln_g[l], ln_b[l], alpha, pe, kv, state_b[:, e])
        else:
            o = l // 2
            po = dict(c=c_hy, w_in=w_in_odd[o].astype(BF16), w_out=w_out_odd[o].astype(BF16),
                      conv_w=h_conv_w[o], conv_b=h_conv_b[o], skip=h_skip[o])
            fargs = (h_ffn_w1[o], h_ffn_b1[o], h_ffn_w2[o], h_ffn_b2[o], h_ffn_w3[o], h_ffn_b3[o],
                     h_freq[o], h_ffn_wout[o])
            spec_x = _hy_spectrum(lx, c_hy, mats_x[0], *fargs)
            spec_y = _hy_spectrum(ly, c_hy, mats_y[0], *fargs)
            x = _odd_layer(x, sc_x, sh_x, gt_x, ln_g[l], ln_b[l], alpha, po, spec_x, mats_x)
            y = _odd_layer(y, sc_y, sh_y, gt_y, ln_g[l], ln_b[l], alpha, po, spec_y, mats_y)
    return (x, y, jnp.stack(new_k, axis=1), jnp.stack(new_v, axis=1), jnp.stack(new_s, axis=1))
```

```python
import functools
import math

import jax
import jax.numpy as jnp
from jax import lax
from jax.experimental import pallas as pl
from jax.experimental.pallas import tpu as pltpu

F32 = jnp.float32
BF16 = jnp.bfloat16

HEAD_DIM = 64
LORA = 64
GRID_W = 64
WIN_H = 8
WIN_W = 16
HY_TARGET = 1e-2
HY_FAST = 0.3
HY_SLOW = 1.5
LN_EPS = 1e-5
GN_EPS = 64e-5
NEG = -1e30

LANES = 128
SUBLANES = 8
VMEM_LIMIT = 56 * 1024 * 1024


def _tile(n, target, mult):
    if n <= target:
        return n
    t = (target // mult) * mult
    while t >= mult:
        if n % t == 0:
            return t
        t -= mult
    return n


def _params(*sem):
    return pltpu.CompilerParams(dimension_semantics=sem, vmem_limit_bytes=VMEM_LIMIT)


def _silu(x):
    return x * jax.nn.sigmoid(x)


def _dot(a, b):
    return jnp.dot(a, b, preferred_element_type=F32)


def _dot_nt(a, b):
    return lax.dot_general(a, b, (((1,), (1,)), ((), ())), preferred_element_type=F32)


def _head_ones():
    r = lax.broadcasted_iota(jnp.int32, (LANES, LANES), 0) // HEAD_DIM
    c = lax.broadcasted_iota(jnp.int32, (LANES, LANES), 1) // HEAD_DIM
    return (r == c).astype(BF16)


def _head_sum(x, ones):
    hi = x.astype(BF16)
    r1 = x - hi.astype(F32)
    mid = r1.astype(BF16)
    lo = (r1 - mid.astype(F32)).astype(BF16)
    return _dot(hi, ones) + _dot(mid, ones) + _dot(lo, ones)


def _ada_kernel(c_ref, w_ref, b_ref, o_ref):
    c = c_ref[...]
    o_ref[...] = _dot(_silu(c).astype(BF16), w_ref[...].astype(BF16)) + b_ref[...]


def _ada_mod(cond, w_ada, b_ada):
    depth, d, n = w_ada.shape
    r = cond.shape[0]
    tn = _tile(n, 768, LANES)
    return pl.pallas_call(
        _ada_kernel,
        grid=(depth, n // tn),
        in_specs=[pl.BlockSpec((r, d), lambda l, j: (0, 0)),
                  pl.BlockSpec((None, d, tn), lambda l, j: (l, 0, j)),
                  pl.BlockSpec((None, 1, tn), lambda l, j: (l, 0, j))],
        out_specs=pl.BlockSpec((None, r, tn), lambda l, j: (l, 0, j)),
        out_shape=jax.ShapeDtypeStruct((depth, r, n), F32),
        compiler_params=_params("parallel", "parallel"),
        name="ada_mod",
    )(cond, w_ada, b_ada.reshape(depth, 1, n))


def _modmm_kernel(x_ref, sc_ref, sh_ref, w_ref, o_ref, h_ref):
    @pl.when(pl.program_id(2) == 0)
    def _():
        h_ref[...] = (x_ref[...] * (1.0 + sc_ref[...]) + sh_ref[...]).astype(BF16)

    o_ref[...] = _dot(h_ref[...], w_ref[...])


def _modmm(x, sc, sh, w):
    b, l, d = x.shape
    n = w.shape[1]
    tl = _tile(l, 1024, SUBLANES)
    tn = _tile(n, 1024, LANES)
    return pl.pallas_call(
        _modmm_kernel,
        grid=(b, l // tl, n // tn),
        in_specs=[pl.BlockSpec((None, tl, d), lambda bb, i, j: (bb, i, 0)),
                  pl.BlockSpec((None, 1, d), lambda bb, i, j: (bb, 0, 0)),
                  pl.BlockSpec((None, 1, d), lambda bb, i, j: (bb, 0, 0)),
                  pl.BlockSpec((d, tn), lambda bb, i, j: (0, j))],
        out_specs=pl.BlockSpec((None, tl, tn), lambda bb, i, j: (bb, i, j)),
        out_shape=jax.ShapeDtypeStruct((b, l, n), F32),
        scratch_shapes=[pltpu.VMEM((tl, d), BF16)],
        compiler_params=_params("parallel", "parallel", "arbitrary"),
        name="mod_in_proj",
    )(x, sc, sh, w)


def _outln_kernel(*refs, nparts, alpha):
    m_refs = refs[:nparts]
    w_refs = refs[nparts:2 * nparts]
    x_ref, gt_ref, g_ref, b_ref, o_ref = refs[2 * nparts:]
    acc = _dot(m_refs[0][...], w_refs[0][...])
    for p in range(1, nparts):
        acc = acc + _dot(m_refs[p][...], w_refs[p][...])
    y = alpha * x_ref[...] + gt_ref[...] * acc
    mu = jnp.mean(y, axis=-1, keepdims=True)
    yc = y - mu
    var = jnp.mean(yc * yc, axis=-1, keepdims=True)
    o_ref[...] = yc * lax.rsqrt(var + LN_EPS) * g_ref[...] + b_ref[...]


def _outln(parts, ws, x, gt, ln_g, ln_b, alpha):
    b, l, d = x.shape
    tl = _tile(l, 512, SUBLANES)
    nparts = len(parts)
    in_specs = [pl.BlockSpec((None, tl, p.shape[2]), lambda bb, i: (bb, i, 0)) for p in parts]
    in_specs += [pl.BlockSpec(w.shape, lambda bb, i: (0, 0)) for w in ws]
    in_specs += [pl.BlockSpec((None, tl, d), lambda bb, i: (bb, i, 0)),
                 pl.BlockSpec((None, 1, d), lambda bb, i: (bb, 0, 0)),
                 pl.BlockSpec((1, d), lambda bb, i: (0, 0)),
                 pl.BlockSpec((1, d), lambda bb, i: (0, 0))]
    return pl.pallas_call(
        functools.partial(_outln_kernel, nparts=nparts, alpha=alpha),
        grid=(b, l // tl),
        in_specs=in_specs,
        out_specs=pl.BlockSpec((None, tl, d), lambda bb, i: (bb, i, 0)),
        out_shape=jax.ShapeDtypeStruct((b, l, d), F32),
        compiler_params=_params("parallel", "parallel"),
        name="out_proj_ln",
    )(*parts, *ws, x, gt, ln_g.reshape(1, d), ln_b.reshape(1, d))


def _conv_specs(tl, tc, l, colblk):
    g = tl // SUBLANES
    last = l // SUBLANES - 1
    return [
        pl.BlockSpec((None, tl, tc), lambda bb, i, j: (bb, i, colblk + j)),
        pl.BlockSpec((None, SUBLANES, tc), lambda bb, i, j: (bb, jnp.maximum(i * g - 1, 0), colblk + j)),
        pl.BlockSpec((None, SUBLANES, tc), lambda bb, i, j: (bb, jnp.minimum((i + 1) * g, last), colblk + j)),
    ]


def _conv3(x_ref, p_ref, n_ref, w, first, last):
    x = x_ref[...]
    tl = x.shape[0]
    row = lax.broadcasted_iota(jnp.int32, x.shape, 0)
    prow = jnp.where(first, 0.0, p_ref[SUBLANES - 1:SUBLANES, :])
    nrow = jnp.where(last, 0.0, n_ref[0:1, :])
    xm = jnp.where(row == 0, prow, pltpu.roll(x, 1, 0))
    xq = jnp.where(row == tl - 1, nrow, pltpu.roll(x, tl - 1, 0))
    return xm * w[0:1, :] + x * w[1:2, :] + xq * w[2:3, :]


def _hy_conv_kernel(x0_ref, x0p_ref, x0n_ref, x1_ref, x1p_ref, x1n_ref, v_ref, vp_ref, vn_ref,
                    w0_ref, w1_ref, w2_ref, b0_ref, b1_ref, b2_ref, x0o_ref, uu_ref, uub_ref):
    first = pl.program_id(1) == 0
    last = pl.program_id(1) == pl.num_programs(1) - 1
    x0 = _conv3(x0_ref, x0p_ref, x0n_ref, w0_ref[...], first, last) + b0_ref[...]
    x1 = _conv3(x1_ref, x1p_ref, x1n_ref, w1_ref[...], first, last) + b1_ref[...]
    v = _conv3(v_ref, vp_ref, vn_ref, w2_ref[...], first, last) + b2_ref[...]
    uu = x1 * v
    x0o_ref[...] = x0
    uu_ref[...] = uu
    uub_ref[...] = uu.astype(BF16)


def _hy_conv(z, conv_w, conv_b, c):
    b, l, _ = z.shape
    tl = _tile(l, 512, SUBLANES)
    tc = _tile(c, 512, LANES)
    nc = c // tc
    in_specs = _conv_specs(tl, tc, l, 0) + _conv_specs(tl, tc, l, nc) + _conv_specs(tl, tc, l, 2 * nc)
    in_specs += [pl.BlockSpec((3, tc), lambda bb, i, j, k=k: (0, k * nc + j)) for k in range(3)]
    in_specs += [pl.BlockSpec((1, tc), lambda bb, i, j, k=k: (0, k * nc + j)) for k in range(3)]
    ospec = pl.BlockSpec((None, tl, tc), lambda bb, i, j: (bb, i, j))
    cb = conv_b.reshape(1, 3 * c)
    return pl.pallas_call(
        _hy_conv_kernel,
        grid=(b, l // tl, nc),
        in_specs=in_specs,
        out_specs=[ospec, ospec, ospec],
        out_shape=[jax.ShapeDtypeStruct((b, l, c), F32), jax.ShapeDtypeStruct((b, l, c), F32),
                   jax.ShapeDtypeStruct((b, l, c), BF16)],
        compiler_params=_params("parallel", "parallel", "parallel"),
        name="hyena_gate_conv",
    )(z, z, z, z, z, z, z, z, z, conv_w, conv_w, conv_w, cb, cb, cb)


def _hy_filter_kernel(feats_ref, t_ref, fw1_ref, fb1_ref, fw2_ref, fb2_ref, fw3_ref, fb3_ref, freq_ref,
                      wf_ref, wb_ref, dl_ref, o_ref):
    freq = freq_ref[...]
    hdn = jnp.sin(freq[0:1, :] * (_dot(feats_ref[...].astype(BF16), fw1_ref[...].astype(BF16)) + fb1_ref[...]))
    hdn = jnp.sin(freq[1:2, :] * (_dot(hdn.astype(BF16), fw2_ref[...].astype(BF16)) + fb2_ref[...]))
    hdn = jnp.sin(freq[2:3, :] * (_dot(hdn.astype(BF16), fw3_ref[...].astype(BF16)) + fb3_ref[...]))
    hb16 = hdn.astype(BF16)
    window = jnp.exp(-t_ref[...] * dl_ref[...])
    h_f = _dot(hb16, wf_ref[...].astype(BF16)) * window
    h_b = _dot(hb16, wb_ref[...].astype(BF16)) * window
    row = lax.broadcasted_iota(jnp.int32, h_b.shape, 0)
    h_b = jnp.where(row == 0, 0.0, h_b)
    norm = jnp.sum(jnp.abs(h_f), axis=0, keepdims=True) + jnp.sum(jnp.abs(h_b), axis=0, keepdims=True)
    inv = 1.0 / norm
    o_ref[0] = ((h_f + h_b) * inv).astype(BF16)
    o_ref[1] = ((h_f - h_b) * inv).astype(BF16)


def _hy_filter(l, c, fw1, fb1, fw2, fb2, fw3, fb3, freq, fwout):
    emb, order = fw1.shape
    bands_n = (emb - 1) // 2
    t = jnp.linspace(0.0, 1.0, l, dtype=F32)[:, None]
    ang = 2.0 * math.pi * jnp.arange(l, dtype=F32)[:, None] / l
    bands = jnp.linspace(1e-4, bands_n - 1, bands_n, dtype=F32)
    feats = jnp.concatenate([t, jnp.cos(ang * bands), -jnp.sin(ang * bands)], axis=-1)
    feats = jnp.pad(feats, ((0, 0), (0, LANES - emb)))
    fw1p = jnp.pad(fw1, ((0, LANES - emb), (0, 0)))
    deltas = jnp.abs(jnp.linspace(math.log(HY_TARGET) / HY_SLOW, math.log(HY_TARGET) / HY_FAST, c, dtype=F32))
    tc = _tile(c, 256, LANES)
    nc = c // tc
    full = lambda shape: pl.BlockSpec(shape, lambda j: tuple(0 for _ in shape))
    return pl.pallas_call(
        _hy_filter_kernel,
        grid=(nc,),
        in_specs=[full((l, LANES)), full((l, 1)), full((LANES, order)), full((1, order)),
                  full((order, order)), full((1, order)), full((order, order)), full((1, order)),
                  full((3, order)),
                  pl.BlockSpec((order, tc), lambda j: (0, j)),
                  pl.BlockSpec((order, tc), lambda j: (0, nc + j)),
                  pl.BlockSpec((1, tc), lambda j: (0, j))],
        out_specs=pl.BlockSpec((2, l, tc), lambda j: (0, 0, j)),
        out_shape=jax.ShapeDtypeStruct((2, l, c), BF16),
        compiler_params=_params("parallel"),
        name="hyena_filter",
    )(feats, t, fw1p, fb1.reshape(1, order), fw2, fb2.reshape(1, order), fw3, fb3.reshape(1, order),
      freq, fwout, fwout, deltas.reshape(1, c))


def _dft_tile(l):
    return _tile(l, 512, SUBLANES)


def _dft_mats(l):
    th = _dft_tile(l)
    nt = l // th
    sq = 1 << (int(math.log2(l)) // 2)
    k = jnp.arange(l, dtype=jnp.int32)
    pa = ((k[:, None] * (jnp.arange(l // sq, dtype=jnp.int32) * sq)[None, :]) % (2 * l)).astype(F32) * (math.pi / l)
    pb = ((k[:, None] * jnp.arange(sq, dtype=jnp.int32)[None, :]) % (2 * l)).astype(F32) * (math.pi / l)
    ca, sa, cb, sb = jnp.cos(pa)[:, :, None], jnp.sin(pa)[:, :, None], jnp.cos(pb)[:, None, :], jnp.sin(pb)[:, None, :]
    cos = (ca * cb - sa * sb).reshape(l, l)
    sin = -(sa * cb + ca * sb).reshape(l, l)
    nyq = jnp.where(k % 2 == 0, 1.0, -1.0).astype(F32)
    fwd_sin = jnp.where((k == 0)[:, None], nyq[None, :], sin)
    fwd = jnp.stack([cos.reshape(nt, th, l), fwd_sin.reshape(nt, th, l)], axis=1).reshape(2 * l, l)
    scale = jnp.where(k == 0, 0.5 / l, 1.0 / l).astype(F32)[None, :]
    inv_sin = jnp.where((k == 0)[None, :], nyq[:, None], sin)
    inv = jnp.stack([(cos * scale).reshape(l, nt, th), (inv_sin * scale).reshape(l, nt, th)], axis=2).reshape(l, 2 * l)
    return fwd.astype(BF16), inv.astype(BF16)


def _dft_plain_kernel(f_ref, x_ref, o_ref):
    o_ref[...] = _dot(f_ref[...], x_ref[...])


def _dft_plain(fwd, x):
    b, l, c = x.shape
    tm = 2 * _dft_tile(l)
    tn = _tile(c, 512, LANES)
    return pl.pallas_call(
        _dft_plain_kernel,
        grid=(b, c // tn, 2 * l // tm),
        in_specs=[pl.BlockSpec((tm, l), lambda bb, j, m: (m, 0)),
                  pl.BlockSpec((None, l, tn), lambda bb, j, m: (bb, 0, j))],
        out_specs=pl.BlockSpec((None, tm, tn), lambda bb, j, m: (bb, m, j)),
        out_shape=jax.ShapeDtypeStruct((b, 2 * l, c), F32),
        compiler_params=_params("parallel", "parallel", "parallel"),
        name="hyena_filter_dft",
    )(fwd, x)


def _dft_fwd_kernel(f_ref, x_ref, hre_ref, him_ref, o_ref, *, th):
    acc = _dot(f_ref[...], x_ref[...])
    re = acc[:th]
    im = acc[th:]
    hre = hre_ref[...]
    him = him_ref[...]
    row = lax.broadcasted_iota(jnp.int32, re.shape, 0)
    dc = jnp.logical_and(pl.program_id(2) == 0, row == 0)
    yre = jnp.where(dc, re * hre, re * hre - im * him)
    yim = jnp.where(dc, im * him, re * him + im * hre)
    o_ref[:th] = yre.astype(BF16)
    o_ref[th:] = yim.astype(BF16)


def _dft_fwd(fwd, x, hre, him):
    b, l, c = x.shape
    th = _dft_tile(l)
    tn = _tile(c, 512, LANES)
    return pl.pallas_call(
        functools.partial(_dft_fwd_kernel, th=th),
        grid=(b, c // tn, l // th),
        in_specs=[pl.BlockSpec((2 * th, l), lambda bb, j, m: (m, 0)),
                  pl.BlockSpec((None, l, tn), lambda bb, j, m: (bb, 0, j)),
                  pl.BlockSpec((th, tn), lambda bb, j, m: (m, j)),
                  pl.BlockSpec((th, tn), lambda bb, j, m: (m, j))],
        out_specs=pl.BlockSpec((None, 2 * th, tn), lambda bb, j, m: (bb, m, j)),
        out_shape=jax.ShapeDtypeStruct((b, 2 * l, c), BF16),
        compiler_params=_params("parallel", "parallel", "parallel"),
        name="hyena_dft_fwd",
    )(fwd, x, hre, him)


def _dft_inv_kernel(g_ref, y_ref, uu_ref, x0_ref, gz_ref, skip_ref, o_ref):
    y = _dot(g_ref[...], y_ref[...]) + uu_ref[...] * skip_ref[...]
    o_ref[...] = (_silu(gz_ref[...]) * (x0_ref[...] * y)).astype(BF16)


def _dft_inv(inv, yf, uu, x0, z, skip):
    b, l, c = uu.shape
    tm = _tile(l, 512, SUBLANES)
    tn = _tile(c, 256, LANES)
    nc = c // tn
    tile = pl.BlockSpec((None, tm, tn), lambda bb, j, i: (bb, i, j))
    return pl.pallas_call(
        _dft_inv_kernel,
        grid=(b, nc, l // tm),
        in_specs=[pl.BlockSpec((tm, 2 * l), lambda bb, j, i: (i, 0)),
                  pl.BlockSpec((None, 2 * l, tn), lambda bb, j, i: (bb, 0, j)),
                  tile, tile,
                  pl.BlockSpec((None, tm, tn), lambda bb, j, i: (bb, i, 3 * nc + j)),
                  pl.BlockSpec((1, tn), lambda bb, j, i: (0, j))],
        out_specs=tile,
        out_shape=jax.ShapeDtypeStruct((b, l, c), BF16),
        compiler_params=_params("parallel", "parallel", "parallel"),
        name="hyena_dft_inv",
    )(inv, yf, uu, x0, z, skip.reshape(1, c))


def _hy_spectrum(l, c, fwd, fw1, fb1, fw2, fb2, fw3, fb3, freq, fwout):
    th = _dft_tile(l)
    nt = l // th
    hpm = _hy_filter(l, c, fw1, fb1, fw2, fb2, fw3, fb3, freq, fwout)
    spec = _dft_plain(fwd, hpm).reshape(2, nt, 2, th, c)
    hre = spec[0, :, 0].reshape(l, c)
    him = spec[1, :, 1].reshape(l, c)
    him = him.at[0].set(spec[0, 0, 1, 0])
    return hre, him


def _lane_mask(shape):
    return lax.broadcasted_iota(jnp.int32, shape, 1) < HEAD_DIM


def _ctx_attn_kernel(q_ref, k_ref, v_ref, ga_ref, o_ref):
    q = q_ref[...]
    k = k_ref[...].astype(BF16)
    v = v_ref[...].astype(BF16)
    m0 = _lane_mask(q.shape)
    scale = HEAD_DIM ** -0.5
    outs = []
    for h in range(2):
        qh = jnp.where(m0 if h == 0 else jnp.logical_not(m0), q, 0.0).astype(BF16)
        s = _dot_nt(qh, k) * scale
        p = jnp.exp(s - jnp.max(s, axis=-1, keepdims=True))
        den = jnp.sum(p, axis=-1, keepdims=True)
        outs.append(_dot(p.astype(BF16), v) / den)
    o = jnp.where(m0, outs[0], outs[1])
    o_ref[...] = (o * _silu(ga_ref[...])).astype(BF16)


def _ctx_attn(z, dh):
    b, l, _ = z.shape
    nb = dh // LANES
    spec = lambda g: pl.BlockSpec((None, l, LANES), lambda bb, hp, g=g: (bb, 0, g * nb + hp))
    return pl.pallas_call(
        _ctx_attn_kernel,
        grid=(b, nb),
        in_specs=[spec(0), spec(1), spec(2), spec(3)],
        out_specs=pl.BlockSpec((None, l, LANES), lambda bb, hp: (bb, 0, hp)),
        out_shape=jax.ShapeDtypeStruct((b, l, dh), BF16),
        compiler_params=_params("parallel", "parallel"),
        name="ctx_attention",
    )(z, z, z, z)


def _na_bias_table(rpb):
    c = jnp.arange(GRID_W)[:, None]
    kc = jnp.arange(GRID_W)[None, :]
    qstart = jnp.clip(c - WIN_W // 2, 0, GRID_W - WIN_W)
    mask = (kc >= qstart) & (kc < qstart + WIN_W)
    cidx = jnp.clip(kc - c, 1 - WIN_W, WIN_W - 1) + WIN_W - 1
    dr = jnp.arange(WIN_H)[:, None] + jnp.arange(WIN_H)[None, :]
    t = rpb[:, dr][..., cidx]
    t = jnp.where(mask, t, NEG)
    nh = rpb.shape[0]
    t = t.transpose(1, 0, 3, 2, 4).reshape(WIN_H, nh // 2, 2 * GRID_W, WIN_H * GRID_W)
    return t.transpose(1, 0, 2, 3)


NA_ROWS_PER_STEP = 4


def _na_kernel(q_ref, k_ref, v_ref, kc_ref, vc_ref, bias_ref, ga_ref, o_ref, kb_ref, vb_ref, *, rows):
    kb_ref[...] = k_ref[...].astype(BF16)
    vb_ref[...] = v_ref[...].astype(BF16)
    kc = kc_ref[...].astype(BF16)
    vc = vc_ref[...].astype(BF16)
    m0 = _lane_mask((GRID_W, LANES))
    scale = HEAD_DIM ** -0.5
    win = WIN_H * GRID_W

    def one_row(r):
        rs = jnp.clip(r - WIN_H // 2, 0, rows - WIN_H)
        d0 = rs - r + WIN_H - 1
        q0 = pl.multiple_of(r * GRID_W, GRID_W)
        k0 = pl.multiple_of(rs * GRID_W, GRID_W)
        q = q_ref[pl.ds(q0, GRID_W), :]
        q2 = jnp.concatenate([jnp.where(m0, q, 0.0), jnp.where(m0, 0.0, q)], axis=0).astype(BF16)
        kw = kb_ref[pl.ds(k0, win), :]
        vw = vb_ref[pl.ds(k0, win), :]
        s_nb = _dot_nt(q2, kw) * scale + bias_ref[d0]
        s_cx = _dot_nt(q2, kc) * scale
        m = jnp.maximum(jnp.max(s_nb, axis=-1, keepdims=True), jnp.max(s_cx, axis=-1, keepdims=True))
        p_nb = jnp.exp(s_nb - m)
        p_cx = jnp.exp(s_cx - m)
        den = jnp.sum(p_nb, axis=-1, keepdims=True) + jnp.sum(p_cx, axis=-1, keepdims=True)
        o2 = (_dot(p_nb.astype(BF16), vw) + _dot(p_cx.astype(BF16), vc)) / den
        o = jnp.where(m0, o2[:GRID_W], o2[GRID_W:])
        ga = ga_ref[pl.ds(q0, GRID_W), :]
        o_ref[pl.ds(q0, GRID_W), :] = (o * _silu(ga)).astype(BF16)

    def body(i, carry):
        for u in range(NA_ROWS_PER_STEP):
            one_row(i * NA_ROWS_PER_STEP + u)
        return carry

    lax.fori_loop(0, rows // NA_ROWS_PER_STEP, body, 0)


def _na_attn(z, dh, k_ctx, v_ctx, bias):
    b, t, _ = z.shape
    rows = t // GRID_W
    assert t % GRID_W == 0 and rows >= WIN_H and rows % NA_ROWS_PER_STEP == 0
    p = k_ctx.shape[1]
    nb = dh // LANES
    spec = lambda g: pl.BlockSpec((None, t, LANES), lambda bb, hp, g=g: (bb, 0, g * nb + hp))
    cspec = pl.BlockSpec((None, p, LANES), lambda bb, hp: (bb, 0, hp))
    return pl.pallas_call(
        functools.partial(_na_kernel, rows=rows),
        grid=(b, nb),
        in_specs=[spec(0), spec(1), spec(2), cspec, cspec,
                  pl.BlockSpec((None, WIN_H, 2 * GRID_W, WIN_H * GRID_W), lambda bb, hp: (hp, 0, 0, 0)),
                  spec(3)],
        out_specs=pl.BlockSpec((None, t, LANES), lambda bb, hp: (bb, 0, hp)),
        out_shape=jax.ShapeDtypeStruct((b, t, dh), BF16),
        scratch_shapes=[pltpu.VMEM((t, LANES), BF16), pltpu.VMEM((t, LANES), BF16)],
        compiler_params=_params("parallel", "parallel"),
        name="na_attention",
    )(z, z, z, k_ctx, v_ctx, bias, z)


N_OPS = 9


def _softplus(x):
    return jnp.maximum(x, 0.0) + jnp.log(1.0 + jnp.exp(-jnp.abs(x)))


def _rwkv_prep_kernel(r_ref, rp_ref, rn_ref, k_ref, kp_ref, kn_ref, v_ref, vp_ref, vn_ref,
                      cwr_ref, cwk_ref, cwv_ref, wdn_ref, adn_ref, wup_ref, aup_ref, w0_ref, a0_ref,
                      kk_ref, ka_ref, rk_ref, o_ref, bonus_ref):
    first = pl.program_id(1) == 0
    last = pl.program_id(1) == pl.num_programs(1) - 1
    r = _conv3(r_ref, rp_ref, rn_ref, cwr_ref[...], first, last)
    k = _conv3(k_ref, kp_ref, kn_ref, cwk_ref[...], first, last)
    v = _conv3(v_ref, vp_ref, vn_ref, cwv_ref[...], first, last)
    ones = _head_ones()
    tw = jnp.tanh(wdn_ref[...]).astype(BF16)
    ad = adn_ref[...].astype(BF16)
    k_a = ka_ref[...]
    kk = k * kk_ref[...]
    nrm = jnp.sqrt(_head_sum(kk * kk, ones))
    kk = kk / jnp.maximum(nrm, 1e-12)
    o_ref[0] = r
    o_ref[1] = v
    o_ref[2] = kk
    kd_sum = None
    for d in range(2):
        w_raw = -_softplus(-(w0_ref[d:d + 1, :] + _dot(tw, wup_ref[d].astype(BF16)))) - 0.5
        decay = jnp.exp(-jnp.exp(w_raw))
        a = jax.nn.sigmoid(a0_ref[d:d + 1, :] + _dot(ad, aup_ref[d].astype(BF16)))
        kd = k * (1.0 + (a - 1.0) * k_a)
        o_ref[3 + d] = decay
        o_ref[5 + d] = kd
        o_ref[7 + d] = kk * a
        kd_sum = kd if kd_sum is None else kd_sum + kd
    bonus_ref[...] = _head_sum(r * kd_sum * rk_ref[...], ones) * v


def _rwkv_prep(z, dh, conv_w, w0, wup, a0, aup, k_k, k_a, r_k):
    b, l, _ = z.shape
    tl = _tile(l, 512, SUBLANES)
    nb = dh // LANES
    base = 4 * nb
    lora_blk = 8 * nb
    in_specs = (_conv_specs(tl, LANES, l, base) + _conv_specs(tl, LANES, l, base + nb)
                + _conv_specs(tl, LANES, l, base + 2 * nb))
    in_specs += [pl.BlockSpec((3, LANES), lambda bb, i, j, g=g: (0, g * nb + j)) for g in range(3)]
    in_specs += [pl.BlockSpec((None, tl, LANES), lambda bb, i, j: (bb, i, lora_blk)),
                 pl.BlockSpec((None, tl, LANES), lambda bb, i, j: (bb, i, lora_blk + 1)),
                 pl.BlockSpec((2, LANES, LANES), lambda bb, i, j: (0, 0, j)),
                 pl.BlockSpec((2, LANES, LANES), lambda bb, i, j: (0, 0, j)),
                 pl.BlockSpec((2, LANES), lambda bb, i, j: (0, j)),
                 pl.BlockSpec((2, LANES), lambda bb, i, j: (0, j)),
                 pl.BlockSpec((1, LANES), lambda bb, i, j: (0, j)),
                 pl.BlockSpec((1, LANES), lambda bb, i, j: (0, j)),
                 pl.BlockSpec((1, LANES), lambda bb, i, j: (0, j))]
    zeros = jnp.zeros((LORA, dh), F32)
    wup_p = jnp.stack([jnp.concatenate([wup[0], zeros]), jnp.concatenate([zeros, wup[1]])])
    aup_p = jnp.stack([jnp.concatenate([aup[0], zeros]), jnp.concatenate([zeros, aup[1]])])
    return pl.pallas_call(
        _rwkv_prep_kernel,
        grid=(b, l // tl, nb),
        in_specs=in_specs,
        out_specs=[pl.BlockSpec((N_OPS, None, tl, LANES), lambda bb, i, j: (0, bb, i, j)),
                   pl.BlockSpec((None, tl, LANES), lambda bb, i, j: (bb, i, j))],
        out_shape=[jax.ShapeDtypeStruct((N_OPS, b, l, dh), F32), jax.ShapeDtypeStruct((b, l, dh), F32)],
        compiler_params=_params("parallel", "parallel", "parallel"),
        name="rwkv_prep",
    )(z, z, z, z, z, z, z, z, z, conv_w, conv_w, conv_w, z, z, wup_p, aup_p, w0, a0,
      k_k.reshape(1, dh), k_a.reshape(1, dh), r_k.reshape(1, dh))


def _scan_kernel(r_ref, v_ref, kk_ref, w_ref, k_ref, b_ref, s0_ref, o_ref, sout_ref, s_ref, *, steps, nblk):
    d = pl.program_id(1)
    i = pl.program_id(2)

    @pl.when(i == 0)
    def _():
        s_ref[...] = s0_ref[...]

    def step(s, carry):
        tt = s + d * (steps - 1 - 2 * s)
        kk = kk_ref[tt]
        w = w_ref[tt]
        kd = k_ref[tt]
        kb = b_ref[tt]
        r = r_ref[tt]
        v = v_ref[tt]
        for n in range(HEAD_DIM):
            sn = s_ref[n]
            sa = -jnp.sum(sn * kk, axis=0, keepdims=True)
            sn = sn * w + sa * kb + v[n:n + 1, :] * kd
            s_ref[n] = sn
            o_ref[tt, n:n + 1, :] = jnp.sum(sn * r, axis=0, keepdims=True)
        return carry

    lax.fori_loop(0, steps, step, 0)

    @pl.when(i == nblk - 1)
    def _():
        sout_ref[...] = s_ref[...]


def _rwkv_scan(ops, s0):
    _, l, n, lanes = ops.shape
    steps = _tile(l, 32, 1)
    nblk = l // steps
    tb = lambda d, i: i + d * (nblk - 1 - 2 * i)
    shared = lambda idx: pl.BlockSpec((None, steps, n, LANES), lambda g, d, i: (idx, tb(d, i), 0, g))
    perdir = lambda idx: pl.BlockSpec((None, steps, n, LANES), lambda g, d, i: (idx + d, tb(d, i), 0, g))
    sspec = pl.BlockSpec((None, n, n, LANES), lambda g, d, i: (d, 0, 0, g))
    return pl.pallas_call(
        functools.partial(_scan_kernel, steps=steps, nblk=nblk),
        grid=(lanes // LANES, 2, nblk),
        in_specs=[shared(0), shared(1), shared(2), perdir(3), perdir(5), perdir(7), sspec],
        out_specs=[pl.BlockSpec((None, steps, n, LANES), lambda g, d, i: (d, tb(d, i), 0, g)), sspec],
        out_shape=[jax.ShapeDtypeStruct((2, l, n, lanes), F32), jax.ShapeDtypeStruct((2, n, n, lanes), F32)],
        scratch_shapes=[pltpu.VMEM((n, n, LANES), F32)],
        compiler_params=_params("parallel", "arbitrary", "arbitrary"),
        name="rwkv_scan",
    )(ops, ops, ops, ops, ops, ops, s0)


def _rwkv_post_kernel(of_ref, ob_ref, bonus_ref, gb_ref, g_ref, b_ref, o_ref):
    ones = _head_ones()
    o = of_ref[...] + ob_ref[...]
    mu = _head_sum(o, ones) * (1.0 / HEAD_DIM)
    oc = o - mu
    var = _head_sum(oc * oc, ones) * (1.0 / HEAD_DIM)
    o = oc * lax.rsqrt(var + GN_EPS) * g_ref[...] + b_ref[...] + bonus_ref[...]
    o_ref[...] = (o * _silu(gb_ref[...])).astype(BF16)


def _rwkv_post(o2, bonus, z, dh, gn_g, gn_b):
    _, b, l, _ = o2.shape
    tl = _tile(l, 512, SUBLANES)
    nb = dh // LANES
    gate_blk = 7 * nb
    return pl.pallas_call(
        _rwkv_post_kernel,
        grid=(b, l // tl, nb),
        in_specs=[pl.BlockSpec((None, None, tl, LANES), lambda bb, i, j: (0, bb, i, j)),
                  pl.BlockSpec((None, None, tl, LANES), lambda bb, i, j: (1, bb, i, j)),
                  pl.BlockSpec((None, tl, LANES), lambda bb, i, j: (bb, i, j)),
                  pl.BlockSpec((None, tl, LANES), lambda bb, i, j: (bb, i, gate_blk + j)),
                  pl.BlockSpec((1, LANES), lambda bb, i, j: (0, j)),
                  pl.BlockSpec((1, LANES), lambda bb, i, j: (0, j))],
        out_specs=pl.BlockSpec((None, tl, LANES), lambda bb, i, j: (bb, i, j)),
        out_shape=jax.ShapeDtypeStruct((b, l, dh), BF16),
        compiler_params=_params("parallel", "parallel", "parallel"),
        name="rwkv_post",
    )(o2, o2, bonus, z, gn_g.reshape(1, dh), gn_b.reshape(1, dh))


def _rwkv_heads(z, dh, pe, s0):
    b, l, _ = z.shape
    nh = dh // HEAD_DIM
    prep, bonus = _rwkv_prep(z, dh, pe["conv_w"], pe["w0"], pe["wup"], pe["a0"], pe["aup"], pe["k_k"], pe["k_a"], pe["r_k"])
    seqs = b * nh
    lanes = -(-seqs // LANES) * LANES
    ops = prep.reshape(N_OPS, b, l, nh, HEAD_DIM).transpose(0, 2, 4, 1, 3).reshape(N_OPS, l, HEAD_DIM, seqs)
    if s0 is None:
        st = jnp.zeros((2, HEAD_DIM, HEAD_DIM, lanes), F32)
    else:
        st = s0.astype(F32).transpose(1, 3, 4, 0, 2).reshape(2, HEAD_DIM, HEAD_DIM, seqs)
    if lanes != seqs:
        ops = jnp.pad(ops, ((0, 0), (0, 0), (0, 0), (0, lanes - seqs)))
        if s0 is not None:
            st = jnp.pad(st, ((0, 0), (0, 0), (0, 0), (0, lanes - seqs)))
    o, s_fin = _rwkv_scan(ops, st)
    o2 = o[..., :seqs].reshape(2, l, HEAD_DIM, b, nh).transpose(0, 3, 1, 4, 2).reshape(2, b, l, dh)
    s_fin = s_fin[..., :seqs].reshape(2, HEAD_DIM, HEAD_DIM, b, nh).transpose(3, 0, 4, 1, 2)
    mixed = _rwkv_post(o2, bonus, z, dh, pe["gn_g"], pe["gn_b"])
    return mixed, s_fin


def _even_layer(x, sc, sh, gt, ln_g, ln_b, alpha, pe, kv_ctx, s0):
    dh = pe["dh"]
    z = _modmm(x, sc, sh, pe["w_in"])
    if kv_ctx is None:
        mixed_a = _ctx_attn(z, dh)
    else:
        mixed_a = _na_attn(z, dh, kv_ctx[0], kv_ctx[1], pe["bias"])
    mixed_b, s_fin = _rwkv_heads(z, dh, pe, s0)
    out = _outln([mixed_a, mixed_b], [pe["w_out"][:dh], pe["w_out"][dh:]], x, gt, ln_g, ln_b, alpha)
    return out, z, s_fin


def _odd_layer(x, sc, sh, gt, ln_g, ln_b, alpha, po, spec, mats):
    c = po["c"]
    z = _modmm(x, sc, sh, po["w_in"])
    x0, uu, uub = _hy_conv(z, po["conv_w"], po["conv_b"], c)
    yf = _dft_fwd(mats[0], uub, spec[0], spec[1])
    mixed = _dft_inv(mats[1], yf, uu, x0, z, po["skip"])
    return _outln([mixed], [po["w_out"]], x, gt, ln_g, ln_b, alpha)


def kernel(x_prompt, x_sample, c, cache_a_k, cache_a_v, state_b, c_ctx, w_ada, b_ada, ln_g, ln_b,
           w_in_even, w_out_even, a_rpb, b_conv_w, b_w0, b_wup, b_a0, b_aup, b_kk, b_ka, b_rk,
           b_gn_g, b_gn_b, w_in_odd, w_out_odd, h_conv_w, h_conv_b, h_ffn_w1, h_ffn_b1, h_ffn_w2,
           h_ffn_b2, h_ffn_w3, h_ffn_b3, h_freq, h_ffn_wout, h_skip):
    depth, d, _ = w_ada.shape
    bx, lx, _ = x_prompt.shape
    by, ly, _ = x_sample.shape
    alpha = (2 * depth) ** 0.25
    dh = b_kk.shape[1]
    nh = dh // HEAD_DIM
    p_len = cache_a_k.shape[2]
    c_hy = h_skip.shape[1]

    rows = 1 + by
    rpad = -(-rows // SUBLANES) * SUBLANES
    cond = jnp.pad(jnp.concatenate([c_ctx[None, :], c], axis=0), ((0, rpad - rows), (0, 0)))
    mods = _ada_mod(cond, w_ada, b_ada)

    x = x_prompt
    y = x_sample
    mats_x = _dft_mats(lx) if depth > 1 else None
    mats_y = _dft_mats(ly) if depth > 1 else None
    new_k, new_v, new_s = [], [], []
    for l in range(depth):
        m = mods[l]
        sh_x, sc_x, gt_x = (jnp.broadcast_to(m[0:1, k * d:(k + 1) * d][None], (bx, 1, d)) for k in range(3))
        sh_y, sc_y, gt_y = (m[1:rows, k * d:(k + 1) * d][:, None, :] for k in range(3))
        if l % 2 == 0:
            e = l // 2
            pe = dict(dh=dh, w_in=w_in_even[e].astype(BF16), w_out=w_out_even[e].astype(BF16),
                      bias=_na_bias_table(a_rpb[e]), conv_w=b_conv_w[e], w0=b_w0[e], wup=b_wup[e], a0=b_a0[e],
                      aup=b_aup[e], k_k=b_kk[e], k_a=b_ka[e], r_k=b_rk[e], gn_g=b_gn_g[e], gn_b=b_gn_b[e])
            x, zx, s_fin = _even_layer(x, sc_x, sh_x, gt_x, ln_g[l], ln_b[l], alpha, pe, None, None)
            new_k.append(zx[..., dh:2 * dh].reshape(bx, lx, nh, HEAD_DIM))
            new_v.append(zx[..., 2 * dh:3 * dh].reshape(bx, lx, nh, HEAD_DIM))
            new_s.append(s_fin)
            kv = (cache_a_k[:, e].reshape(by, p_len, dh), cache_a_v[:, e].reshape(by, p_len, dh))
            y, _, _ = _even_layer(y, sc_y, sh_y, gt_y, ln_g[l], ln_b[l], alpha, pe, kv, state_b[:, e])
        else:
            o = l // 2
            po = dict(c=c_hy, w_in=w_in_odd[o].astype(BF16), w_out=w_out_odd[o].astype(BF16),
                      conv_w=h_conv_w[o], conv_b=h_conv_b[o], skip=h_skip[o])
            fargs = (h_ffn_w1[o], h_ffn_b1[o], h_ffn_w2[o], h_ffn_b2[o], h_ffn_w3[o], h_ffn_b3[o],
                     h_freq[o], h_ffn_wout[o])
            spec_x = _hy_spectrum(lx, c_hy, mats_x[0], *fargs)
            spec_y = _hy_spectrum(ly, c_hy, mats_y[0], *fargs)
            x = _odd_layer(x, sc_x, sh_x, gt_x, ln_g[l], ln_b[l], alpha, po, spec_x, mats_x)
            y = _odd_layer(y, sc_y, sh_y, gt_y, ln_g[l], ln_b[l], alpha, po, spec_y, mats_y)
    return (x, y, jnp.stack(new_k, axis=1), jnp.stack(new_v, axis=1), jnp.stack(new_s, axis=1))
```

```python
import functools
import math

import jax
import jax.numpy as jnp
from jax import lax
from jax.experimental import pallas as pl
from jax.experimental.pallas import tpu as pltpu

F32 = jnp.float32
BF16 = jnp.bfloat16

HEAD_DIM = 64
LORA = 64
GRID_W = 64
WIN_H = 8
WIN_W = 16
HY_TARGET = 1e-2
HY_FAST = 0.3
HY_SLOW = 1.5
LN_EPS = 1e-5
GN_EPS = 64e-5
NEG = -1e30

LANES = 128
SUBLANES = 8
VMEM_LIMIT = 56 * 1024 * 1024


def _tile(n, target, mult):
    if n <= target:
        return n
    t = (target // mult) * mult
    while t >= mult:
        if n % t == 0:
            return t
        t -= mult
    return n


def _params(*sem):
    return pltpu.CompilerParams(dimension_semantics=sem, vmem_limit_bytes=VMEM_LIMIT)


def _silu(x):
    return x * jax.nn.sigmoid(x)


def _dot(a, b):
    return jnp.dot(a, b, preferred_element_type=F32)


def _dot_nt(a, b):
    return lax.dot_general(a, b, (((1,), (1,)), ((), ())), preferred_element_type=F32)


def _run_lockstep(gens):
    results = [None] * len(gens)
    active = list(range(len(gens)))
    while active:
        for p in list(active):
            try:
                next(gens[p])
            except StopIteration as done:
                results[p] = done.value
                active.remove(p)
    return results


def _head_ones():
    r = lax.broadcasted_iota(jnp.int32, (LANES, LANES), 0) // HEAD_DIM
    c = lax.broadcasted_iota(jnp.int32, (LANES, LANES), 1) // HEAD_DIM
    return (r == c).astype(BF16)


def _head_sum(x, ones):
    hi = x.astype(BF16)
    r1 = x - hi.astype(F32)
    mid = r1.astype(BF16)
    lo = (r1 - mid.astype(F32)).astype(BF16)
    return _dot(hi, ones) + _dot(mid, ones) + _dot(lo, ones)


def _ada_kernel(c_ref, w_ref, b_ref, o_ref):
    c = c_ref[...]
    o_ref[...] = _dot(_silu(c).astype(BF16), w_ref[...].astype(BF16)) + b_ref[...]


def _ada_mod(cond, w_ada, b_ada):
    depth, d, n = w_ada.shape
    r = cond.shape[0]
    tn = _tile(n, 768, LANES)
    return pl.pallas_call(
        _ada_kernel,
        grid=(depth, n // tn),
        in_specs=[pl.BlockSpec((r, d), lambda l, j: (0, 0)),
                  pl.BlockSpec((None, d, tn), lambda l, j: (l, 0, j)),
                  pl.BlockSpec((None, 1, tn), lambda l, j: (l, 0, j))],
        out_specs=pl.BlockSpec((None, r, tn), lambda l, j: (l, 0, j)),
        out_shape=jax.ShapeDtypeStruct((depth, r, n), F32),
        compiler_params=_params("parallel", "parallel"),
        name="ada_mod",
    )(cond, w_ada, b_ada.reshape(depth, 1, n))


def _modmm_kernel(x_ref, sc_ref, sh_ref, w_ref, o_ref, h_ref):
    @pl.when(pl.program_id(2) == 0)
    def _():
        h_ref[...] = (x_ref[...] * (1.0 + sc_ref[...]) + sh_ref[...]).astype(BF16)

    o_ref[...] = _dot(h_ref[...], w_ref[...])


def _modmm(x, sc, sh, w):
    b, l, d = x.shape
    n = w.shape[1]
    tl = _tile(l, 1024, SUBLANES)
    tn = _tile(n, 1024, LANES)
    return pl.pallas_call(
        _modmm_kernel,
        grid=(b, l // tl, n // tn),
        in_specs=[pl.BlockSpec((None, tl, d), lambda bb, i, j: (bb, i, 0)),
                  pl.BlockSpec((None, 1, d), lambda bb, i, j: (bb, 0, 0)),
                  pl.BlockSpec((None, 1, d), lambda bb, i, j: (bb, 0, 0)),
                  pl.BlockSpec((d, tn), lambda bb, i, j: (0, j))],
        out_specs=pl.BlockSpec((None, tl, tn), lambda bb, i, j: (bb, i, j)),
        out_shape=jax.ShapeDtypeStruct((b, l, n), F32),
        scratch_shapes=[pltpu.VMEM((tl, d), BF16)],
        compiler_params=_params("parallel", "parallel", "arbitrary"),
        name="mod_in_proj",
    )(x, sc, sh, w)


def _outln_kernel(*refs, nparts, alpha):
    m_refs = refs[:nparts]
    w_refs = refs[nparts:2 * nparts]
    x_ref, gt_ref, g_ref, b_ref, o_ref = refs[2 * nparts:]
    acc = _dot(m_refs[0][...], w_refs[0][...])
    for p in range(1, nparts):
        acc = acc + _dot(m_refs[p][...], w_refs[p][...])
    y = alpha * x_ref[...] + gt_ref[...] * acc
    mu = jnp.mean(y, axis=-1, keepdims=True)
    yc = y - mu
    var = jnp.mean(yc * yc, axis=-1, keepdims=True)
    o_ref[...] = yc * lax.rsqrt(var + LN_EPS) * g_ref[...] + b_ref[...]


def _outln(parts, ws, x, gt, ln_g, ln_b, alpha):
    b, l, d = x.shape
    tl = _tile(l, 512, SUBLANES)
    nparts = len(parts)
    in_specs = [pl.BlockSpec((None, tl, p.shape[2]), lambda bb, i: (bb, i, 0)) for p in parts]
    in_specs += [pl.BlockSpec(w.shape, lambda bb, i: (0, 0)) for w in ws]
    in_specs += [pl.BlockSpec((None, tl, d), lambda bb, i: (bb, i, 0)),
                 pl.BlockSpec((None, 1, d), lambda bb, i: (bb, 0, 0)),
                 pl.BlockSpec((1, d), lambda bb, i: (0, 0)),
                 pl.BlockSpec((1, d), lambda bb, i: (0, 0))]
    return pl.pallas_call(
        functools.partial(_outln_kernel, nparts=nparts, alpha=alpha),
        grid=(b, l // tl),
        in_specs=in_specs,
        out_specs=pl.BlockSpec((None, tl, d), lambda bb, i: (bb, i, 0)),
        out_shape=jax.ShapeDtypeStruct((b, l, d), F32),
        compiler_params=_params("parallel", "parallel"),
        name="out_proj_ln",
    )(*parts, *ws, x, gt, ln_g.reshape(1, d), ln_b.reshape(1, d))


def _conv_specs(tl, tc, l, colblk):
    g = tl // SUBLANES
    last = l // SUBLANES - 1
    return [
        pl.BlockSpec((None, tl, tc), lambda bb, i, j: (bb, i, colblk + j)),
        pl.BlockSpec((None, SUBLANES, tc), lambda bb, i, j: (bb, jnp.maximum(i * g - 1, 0), colblk + j)),
        pl.BlockSpec((None, SUBLANES, tc), lambda bb, i, j: (bb, jnp.minimum((i + 1) * g, last), colblk + j)),
    ]


def _conv3(x_ref, p_ref, n_ref, w, first, last):
    x = x_ref[...]
    tl = x.shape[0]
    row = lax.broadcasted_iota(jnp.int32, x.shape, 0)
    prow = jnp.where(first, 0.0, p_ref[SUBLANES - 1:SUBLANES, :])
    nrow = jnp.where(last, 0.0, n_ref[0:1, :])
    xm = jnp.where(row == 0, prow, pltpu.roll(x, 1, 0))
    xq = jnp.where(row == tl - 1, nrow, pltpu.roll(x, tl - 1, 0))
    return xm * w[0:1, :] + x * w[1:2, :] + xq * w[2:3, :]


def _hy_conv_kernel(x0_ref, x0p_ref, x0n_ref, x1_ref, x1p_ref, x1n_ref, v_ref, vp_ref, vn_ref,
                    w0_ref, w1_ref, w2_ref, b0_ref, b1_ref, b2_ref, x0o_ref, uu_ref, uub_ref):
    first = pl.program_id(1) == 0
    last = pl.program_id(1) == pl.num_programs(1) - 1
    x0 = _conv3(x0_ref, x0p_ref, x0n_ref, w0_ref[...], first, last) + b0_ref[...]
    x1 = _conv3(x1_ref, x1p_ref, x1n_ref, w1_ref[...], first, last) + b1_ref[...]
    v = _conv3(v_ref, vp_ref, vn_ref, w2_ref[...], first, last) + b2_ref[...]
    uu = x1 * v
    x0o_ref[...] = x0
    uu_ref[...] = uu
    uub_ref[...] = uu.astype(BF16)


def _hy_conv(z, conv_w, conv_b, c):
    b, l, _ = z.shape
    tl = _tile(l, 512, SUBLANES)
    tc = _tile(c, 512, LANES)
    nc = c // tc
    in_specs = _conv_specs(tl, tc, l, 0) + _conv_specs(tl, tc, l, nc) + _conv_specs(tl, tc, l, 2 * nc)
    in_specs += [pl.BlockSpec((3, tc), lambda bb, i, j, k=k: (0, k * nc + j)) for k in range(3)]
    in_specs += [pl.BlockSpec((1, tc), lambda bb, i, j, k=k: (0, k * nc + j)) for k in range(3)]
    ospec = pl.BlockSpec((None, tl, tc), lambda bb, i, j: (bb, i, j))
    cb = conv_b.reshape(1, 3 * c)
    return pl.pallas_call(
        _hy_conv_kernel,
        grid=(b, l // tl, nc),
        in_specs=in_specs,
        out_specs=[ospec, ospec, ospec],
        out_shape=[jax.ShapeDtypeStruct((b, l, c), F32), jax.ShapeDtypeStruct((b, l, c), F32),
                   jax.ShapeDtypeStruct((b, l, c), BF16)],
        compiler_params=_params("parallel", "parallel", "parallel"),
        name="hyena_gate_conv",
    )(z, z, z, z, z, z, z, z, z, conv_w, conv_w, conv_w, cb, cb, cb)


def _hy_filter_kernel(feats_ref, t_ref, fw1_ref, fb1_ref, fw2_ref, fb2_ref, fw3_ref, fb3_ref, freq_ref,
                      wf_ref, wb_ref, dl_ref, o_ref):
    freq = freq_ref[...]
    hdn = jnp.sin(freq[0:1, :] * (_dot(feats_ref[...].astype(BF16), fw1_ref[...].astype(BF16)) + fb1_ref[...]))
    hdn = jnp.sin(freq[1:2, :] * (_dot(hdn.astype(BF16), fw2_ref[...].astype(BF16)) + fb2_ref[...]))
    hdn = jnp.sin(freq[2:3, :] * (_dot(hdn.astype(BF16), fw3_ref[...].astype(BF16)) + fb3_ref[...]))
    hb16 = hdn.astype(BF16)
    window = jnp.exp(-t_ref[...] * dl_ref[...])
    h_f = _dot(hb16, wf_ref[...].astype(BF16)) * window
    h_b = _dot(hb16, wb_ref[...].astype(BF16)) * window
    row = lax.broadcasted_iota(jnp.int32, h_b.shape, 0)
    h_b = jnp.where(row == 0, 0.0, h_b)
    norm = jnp.sum(jnp.abs(h_f), axis=0, keepdims=True) + jnp.sum(jnp.abs(h_b), axis=0, keepdims=True)
    inv = 1.0 / norm
    o_ref[0] = ((h_f + h_b) * inv).astype(BF16)
    o_ref[1] = ((h_f - h_b) * inv).astype(BF16)


def _hy_filter(l, c, fw1, fb1, fw2, fb2, fw3, fb3, freq, fwout):
    emb, order = fw1.shape
    bands_n = (emb - 1) // 2
    t = jnp.linspace(0.0, 1.0, l, dtype=F32)[:, None]
    ang = 2.0 * math.pi * jnp.arange(l, dtype=F32)[:, None] / l
    bands = jnp.linspace(1e-4, bands_n - 1, bands_n, dtype=F32)
    feats = jnp.concatenate([t, jnp.cos(ang * bands), -jnp.sin(ang * bands)], axis=-1)
    feats = jnp.pad(feats, ((0, 0), (0, LANES - emb)))
    fw1p = jnp.pad(fw1, ((0, LANES - emb), (0, 0)))
    deltas = jnp.abs(jnp.linspace(math.log(HY_TARGET) / HY_SLOW, math.log(HY_TARGET) / HY_FAST, c, dtype=F32))
    tc = _tile(c, 256, LANES)
    nc = c // tc
    full = lambda shape: pl.BlockSpec(shape, lambda j: tuple(0 for _ in shape))
    return pl.pallas_call(
        _hy_filter_kernel,
        grid=(nc,),
        in_specs=[full((l, LANES)), full((l, 1)), full((LANES, order)), full((1, order)),
                  full((order, order)), full((1, order)), full((order, order)), full((1, order)),
                  full((3, order)),
                  pl.BlockSpec((order, tc), lambda j: (0, j)),
                  pl.BlockSpec((order, tc), lambda j: (0, nc + j)),
                  pl.BlockSpec((1, tc), lambda j: (0, j))],
        out_specs=pl.BlockSpec((2, l, tc), lambda j: (0, 0, j)),
        out_shape=jax.ShapeDtypeStruct((2, l, c), BF16),
        compiler_params=_params("parallel"),
        name="hyena_filter",
    )(feats, t, fw1p, fb1.reshape(1, order), fw2, fb2.reshape(1, order), fw3, fb3.reshape(1, order),
      freq, fwout, fwout, deltas.reshape(1, c))


def _dft_tile(l):
    return _tile(l, 512, SUBLANES)


def _dft_mats(l):
    th = _dft_tile(l)
    nt = l // th
    sq = 1 << (int(math.log2(l)) // 2)
    k = jnp.arange(l, dtype=jnp.int32)
    pa = ((k[:, None] * (jnp.arange(l // sq, dtype=jnp.int32) * sq)[None, :]) % (2 * l)).astype(F32) * (math.pi / l)
    pb = ((k[:, None] * jnp.arange(sq, dtype=jnp.int32)[None, :]) % (2 * l)).astype(F32) * (math.pi / l)
    ca, sa, cb, sb = jnp.cos(pa)[:, :, None], jnp.sin(pa)[:, :, None], jnp.cos(pb)[:, None, :], jnp.sin(pb)[:, None, :]
    cos = (ca * cb - sa * sb).reshape(l, l)
    sin = -(sa * cb + ca * sb).reshape(l, l)
    nyq = jnp.where(k % 2 == 0, 1.0, -1.0).astype(F32)
    fwd_sin = jnp.where((k == 0)[:, None], nyq[None, :], sin)
    fwd = jnp.stack([cos.reshape(nt, th, l), fwd_sin.reshape(nt, th, l)], axis=1).reshape(2 * l, l)
    scale = jnp.where(k == 0, 0.5 / l, 1.0 / l).astype(F32)[None, :]
    inv_sin = jnp.where((k == 0)[None, :], nyq[:, None], sin)
    inv = jnp.stack([(cos * scale).reshape(l, nt, th), (inv_sin * scale).reshape(l, nt, th)], axis=2).reshape(l, 2 * l)
    return fwd.astype(BF16), inv.astype(BF16)


def _dft_plain_kernel(f_ref, x_ref, o_ref):
    o_ref[...] = _dot(f_ref[...], x_ref[...])


def _dft_plain(fwd, x):
    b, l, c = x.shape
    tm = 2 * _dft_tile(l)
    tn = _tile(c, 512, LANES)
    return pl.pallas_call(
        _dft_plain_kernel,
        grid=(b, c // tn, 2 * l // tm),
        in_specs=[pl.BlockSpec((tm, l), lambda bb, j, m: (m, 0)),
                  pl.BlockSpec((None, l, tn), lambda bb, j, m: (bb, 0, j))],
        out_specs=pl.BlockSpec((None, tm, tn), lambda bb, j, m: (bb, m, j)),
        out_shape=jax.ShapeDtypeStruct((b, 2 * l, c), F32),
        compiler_params=_params("parallel", "parallel", "parallel"),
        name="hyena_filter_dft",
    )(fwd, x)


def _dft_fwd_kernel(f_ref, x_ref, hre_ref, him_ref, o_ref, *, th):
    acc = _dot(f_ref[...], x_ref[...])
    re = acc[:th]
    im = acc[th:]
    hre = hre_ref[...]
    him = him_ref[...]
    row = lax.broadcasted_iota(jnp.int32, re.shape, 0)
    dc = jnp.logical_and(pl.program_id(2) == 0, row == 0)
    yre = jnp.where(dc, re * hre, re * hre - im * him)
    yim = jnp.where(dc, im * him, re * him + im * hre)
    o_ref[:th] = yre.astype(BF16)
    o_ref[th:] = yim.astype(BF16)


def _dft_fwd(fwd, x, hre, him):
    b, l, c = x.shape
    th = _dft_tile(l)
    tn = _tile(c, 512, LANES)
    return pl.pallas_call(
        functools.partial(_dft_fwd_kernel, th=th),
        grid=(b, c // tn, l // th),
        in_specs=[pl.BlockSpec((2 * th, l), lambda bb, j, m: (m, 0)),
                  pl.BlockSpec((None, l, tn), lambda bb, j, m: (bb, 0, j)),
                  pl.BlockSpec((th, tn), lambda bb, j, m: (m, j)),
                  pl.BlockSpec((th, tn), lambda bb, j, m: (m, j))],
        out_specs=pl.BlockSpec((None, 2 * th, tn), lambda bb, j, m: (bb, m, j)),
        out_shape=jax.ShapeDtypeStruct((b, 2 * l, c), BF16),
        compiler_params=_params("parallel", "parallel", "parallel"),
        name="hyena_dft_fwd",
    )(fwd, x, hre, him)


def _dft_inv_kernel(g_ref, y_ref, uu_ref, x0_ref, gz_ref, skip_ref, o_ref):
    y = _dot(g_ref[...], y_ref[...]) + uu_ref[...] * skip_ref[...]
    o_ref[...] = (_silu(gz_ref[...]) * (x0_ref[...] * y)).astype(BF16)


def _dft_inv(inv, yf, uu, x0, z, skip):
    b, l, c = uu.shape
    tm = _tile(l, 512, SUBLANES)
    tn = _tile(c, 512, LANES)
    nc = c // tn
    tile = pl.BlockSpec((None, tm, tn), lambda bb, j, i: (bb, i, j))
    return pl.pallas_call(
        _dft_inv_kernel,
        grid=(b, nc, l // tm),
        in_specs=[pl.BlockSpec((tm, 2 * l), lambda bb, j, i: (i, 0)),
                  pl.BlockSpec((None, 2 * l, tn), lambda bb, j, i: (bb, 0, j)),
                  tile, tile,
                  pl.BlockSpec((None, tm, tn), lambda bb, j, i: (bb, i, 3 * nc + j)),
                  pl.BlockSpec((1, tn), lambda bb, j, i: (0, j))],
        out_specs=tile,
        out_shape=jax.ShapeDtypeStruct((b, l, c), BF16),
        compiler_params=_params("parallel", "parallel", "parallel"),
        name="hyena_dft_inv",
    )(inv, yf, uu, x0, z, skip.reshape(1, c))


def _hy_spectrum(l, c, fwd, fw1, fb1, fw2, fb2, fw3, fb3, freq, fwout):
    th = _dft_tile(l)
    nt = l // th
    hpm = _hy_filter(l, c, fw1, fb1, fw2, fb2, fw3, fb3, freq, fwout)
    spec = _dft_plain(fwd, hpm).reshape(2, nt, 2, th, c)
    hre = spec[0, :, 0].reshape(l, c)
    him = spec[1, :, 1].reshape(l, c)
    him = him.at[0].set(spec[0, 0, 1, 0])
    return hre, him


def _lane_mask(shape):
    return lax.broadcasted_iota(jnp.int32, shape, 1) < HEAD_DIM


def _ctx_attn_kernel(q_ref, k_ref, v_ref, ga_ref, o_ref):
    q = q_ref[...]
    k = k_ref[...].astype(BF16)
    v = v_ref[...].astype(BF16)
    m0 = _lane_mask(q.shape)
    scale = HEAD_DIM ** -0.5
    outs = []
    for h in range(2):
        qh = jnp.where(m0 if h == 0 else jnp.logical_not(m0), q, 0.0).astype(BF16)
        s = _dot_nt(qh, k) * scale
        p = jnp.exp(s - jnp.max(s, axis=-1, keepdims=True))
        den = jnp.sum(p, axis=-1, keepdims=True)
        outs.append(_dot(p.astype(BF16), v) / den)
    o = jnp.where(m0, outs[0], outs[1])
    o_ref[...] = (o * _silu(ga_ref[...])).astype(BF16)


def _ctx_attn(z, dh):
    b, l, _ = z.shape
    nb = dh // LANES
    spec = lambda g: pl.BlockSpec((None, l, LANES), lambda bb, hp, g=g: (bb, 0, g * nb + hp))
    return pl.pallas_call(
        _ctx_attn_kernel,
        grid=(b, nb),
        in_specs=[spec(0), spec(1), spec(2), spec(3)],
        out_specs=pl.BlockSpec((None, l, LANES), lambda bb, hp: (bb, 0, hp)),
        out_shape=jax.ShapeDtypeStruct((b, l, dh), BF16),
        compiler_params=_params("parallel", "parallel"),
        name="ctx_attention",
    )(z, z, z, z)


def _na_bias_table(rpb):
    c = jnp.arange(GRID_W)[:, None]
    kc = jnp.arange(GRID_W)[None, :]
    qstart = jnp.clip(c - WIN_W // 2, 0, GRID_W - WIN_W)
    mask = (kc >= qstart) & (kc < qstart + WIN_W)
    cidx = jnp.clip(kc - c, 1 - WIN_W, WIN_W - 1) + WIN_W - 1
    dr = jnp.arange(WIN_H)[:, None] + jnp.arange(WIN_H)[None, :]
    t = rpb[:, dr][..., cidx]
    t = jnp.where(mask, t, NEG)
    nh = rpb.shape[0]
    t = t.transpose(1, 0, 3, 2, 4).reshape(WIN_H, nh // 2, 2 * GRID_W, WIN_H * GRID_W)
    return t.transpose(1, 0, 2, 3)


NA_ROWS_PER_STEP = 4


def _na_kernel(q_ref, k_ref, v_ref, kc_ref, vc_ref, bias_ref, ga_ref, o_ref, kb_ref, vb_ref, *, rows):
    kb_ref[...] = k_ref[...].astype(BF16)
    vb_ref[...] = v_ref[...].astype(BF16)
    kc = kc_ref[...].astype(BF16)
    vc = vc_ref[...].astype(BF16)
    m0 = _lane_mask((GRID_W, LANES))
    scale = HEAD_DIM ** -0.5
    win = WIN_H * GRID_W

    def one_row(r):
        rs = jnp.clip(r - WIN_H // 2, 0, rows - WIN_H)
        d0 = rs - r + WIN_H - 1
        q0 = pl.multiple_of(r * GRID_W, GRID_W)
        k0 = pl.multiple_of(rs * GRID_W, GRID_W)
        q = q_ref[pl.ds(q0, GRID_W), :]
        q2 = jnp.concatenate([jnp.where(m0, q, 0.0), jnp.where(m0, 0.0, q)], axis=0).astype(BF16)
        kw = kb_ref[pl.ds(k0, win), :]
        vw = vb_ref[pl.ds(k0, win), :]
        s_nb = _dot_nt(q2, kw)
        s_cx = _dot_nt(q2, kc)
        yield
        s_nb = s_nb * scale + bias_ref[d0]
        s_cx = s_cx * scale
        m = jnp.maximum(jnp.max(s_nb, axis=-1, keepdims=True), jnp.max(s_cx, axis=-1, keepdims=True))
        p_nb = jnp.exp(s_nb - m)
        p_cx = jnp.exp(s_cx - m)
        den = jnp.sum(p_nb, axis=-1, keepdims=True) + jnp.sum(p_cx, axis=-1, keepdims=True)
        o2 = _dot(p_nb.astype(BF16), vw) + _dot(p_cx.astype(BF16), vc)
        yield
        o2 = o2 / den
        o = jnp.where(m0, o2[:GRID_W], o2[GRID_W:])
        ga = ga_ref[pl.ds(q0, GRID_W), :]
        o_ref[pl.ds(q0, GRID_W), :] = (o * _silu(ga)).astype(BF16)

    def body(i, carry):
        _run_lockstep([one_row(i * NA_ROWS_PER_STEP + u) for u in range(NA_ROWS_PER_STEP)])
        return carry

    lax.fori_loop(0, rows // NA_ROWS_PER_STEP, body, 0)


def _na_attn(z, dh, k_ctx, v_ctx, bias):
    b, t, _ = z.shape
    rows = t // GRID_W
    assert t % GRID_W == 0 and rows >= WIN_H and rows % NA_ROWS_PER_STEP == 0
    p = k_ctx.shape[1]
    nb = dh // LANES
    spec = lambda g: pl.BlockSpec((None, t, LANES), lambda bb, hp, g=g: (bb, 0, g * nb + hp))
    cspec = pl.BlockSpec((None, p, LANES), lambda bb, hp: (bb, 0, hp))
    return pl.pallas_call(
        functools.partial(_na_kernel, rows=rows),
        grid=(b, nb),
        in_specs=[spec(0), spec(1), spec(2), cspec, cspec,
                  pl.BlockSpec((None, WIN_H, 2 * GRID_W, WIN_H * GRID_W), lambda bb, hp: (hp, 0, 0, 0)),
                  spec(3)],
        out_specs=pl.BlockSpec((None, t, LANES), lambda bb, hp: (bb, 0, hp)),
        out_shape=jax.ShapeDtypeStruct((b, t, dh), BF16),
        scratch_shapes=[pltpu.VMEM((t, LANES), BF16), pltpu.VMEM((t, LANES), BF16)],
        compiler_params=_params("parallel", "parallel"),
        name="na_attention",
    )(z, z, z, k_ctx, v_ctx, bias, z)


N_OPS = 9


def _softplus(x):
    return jnp.maximum(x, 0.0) + jnp.log(1.0 + jnp.exp(-jnp.abs(x)))


def _rwkv_prep_kernel(r_ref, rp_ref, rn_ref, k_ref, kp_ref, kn_ref, v_ref, vp_ref, vn_ref,
                      cwr_ref, cwk_ref, cwv_ref, wdn_ref, adn_ref, wup_ref, aup_ref, w0_ref, a0_ref,
                      kk_ref, ka_ref, rk_ref, o_ref, bonus_ref):
    first = pl.program_id(1) == 0
    last = pl.program_id(1) == pl.num_programs(1) - 1
    r = _conv3(r_ref, rp_ref, rn_ref, cwr_ref[...], first, last)
    k = _conv3(k_ref, kp_ref, kn_ref, cwk_ref[...], first, last)
    v = _conv3(v_ref, vp_ref, vn_ref, cwv_ref[...], first, last)
    ones = _head_ones()
    tw = jnp.tanh(wdn_ref[...]).astype(BF16)
    ad = adn_ref[...].astype(BF16)
    k_a = ka_ref[...]
    kk = k * kk_ref[...]
    nrm = jnp.sqrt(_head_sum(kk * kk, ones))
    kk = kk / jnp.maximum(nrm, 1e-12)
    o_ref[0] = r
    o_ref[1] = v
    o_ref[2] = kk
    kd_sum = None
    for d in range(2):
        w_raw = -_softplus(-(w0_ref[d:d + 1, :] + _dot(tw, wup_ref[d].astype(BF16)))) - 0.5
        a = jax.nn.sigmoid(a0_ref[d:d + 1, :] + _dot(ad, aup_ref[d].astype(BF16)))
        kd = k * (1.0 + (a - 1.0) * k_a)
        o_ref[3 + d] = -jnp.exp(w_raw)
        o_ref[5 + d] = kd
        o_ref[7 + d] = kk * a
        kd_sum = kd if kd_sum is None else kd_sum + kd
    bonus_ref[...] = _head_sum(r * kd_sum * rk_ref[...], ones) * v


def _rwkv_prep(z, dh, conv_w, w0, wup, a0, aup, k_k, k_a, r_k):
    b, l, _ = z.shape
    tl = _tile(l, 512, SUBLANES)
    nb = dh // LANES
    base = 4 * nb
    lora_blk = 8 * nb
    in_specs = (_conv_specs(tl, LANES, l, base) + _conv_specs(tl, LANES, l, base + nb)
                + _conv_specs(tl, LANES, l, base + 2 * nb))
    in_specs += [pl.BlockSpec((3, LANES), lambda bb, i, j, g=g: (0, g * nb + j)) for g in range(3)]
    in_specs += [pl.BlockSpec((None, tl, LANES), lambda bb, i, j: (bb, i, lora_blk)),
                 pl.BlockSpec((None, tl, LANES), lambda bb, i, j: (bb, i, lora_blk + 1)),
                 pl.BlockSpec((2, LANES, LANES), lambda bb, i, j: (0, 0, j)),
                 pl.BlockSpec((2, LANES, LANES), lambda bb, i, j: (0, 0, j)),
                 pl.BlockSpec((2, LANES), lambda bb, i, j: (0, j)),
                 pl.BlockSpec((2, LANES), lambda bb, i, j: (0, j)),
                 pl.BlockSpec((1, LANES), lambda bb, i, j: (0, j)),
                 pl.BlockSpec((1, LANES), lambda bb, i, j: (0, j)),
                 pl.BlockSpec((1, LANES), lambda bb, i, j: (0, j))]
    zeros = jnp.zeros((LORA, dh), F32)
    wup_p = jnp.stack([jnp.concatenate([wup[0], zeros]), jnp.concatenate([zeros, wup[1]])])
    aup_p = jnp.stack([jnp.concatenate([aup[0], zeros]), jnp.concatenate([zeros, aup[1]])])
    return pl.pallas_call(
        _rwkv_prep_kernel,
        grid=(b, l // tl, nb),
        in_specs=in_specs,
        out_specs=[pl.BlockSpec((N_OPS, None, tl, LANES), lambda bb, i, j: (0, bb, i, j)),
                   pl.BlockSpec((None, tl, LANES), lambda bb, i, j: (bb, i, j))],
        out_shape=[jax.ShapeDtypeStruct((N_OPS, b, l, dh), F32), jax.ShapeDtypeStruct((b, l, dh), F32)],
        compiler_params=_params("parallel", "parallel", "parallel"),
        name="rwkv_prep",
    )(z, z, z, z, z, z, z, z, z, conv_w, conv_w, conv_w, z, z, wup_p, aup_p, w0, a0,
      k_k.reshape(1, dh), k_a.reshape(1, dh), r_k.reshape(1, dh))


SCAN_CHUNK = 64


def _split3(x):
    hi = x.astype(BF16)
    r1 = x - hi.astype(F32)
    mid = r1.astype(BF16)
    return hi, mid, (r1 - mid.astype(F32)).astype(BF16)


def _stack_heads(x, m0):
    return jnp.concatenate([jnp.where(m0, x, 0.0), jnp.where(m0, 0.0, x)], axis=0).astype(BF16)


def _scan_pair(lw, r, v, kk, kd, kb, s2, tri, strict2, incl2, m0, cm0, bd):
    c = lw.shape[0]
    hi, mid, lo = _split3(lw)
    a = _dot(tri, hi) + _dot(tri, mid) + _dot(tri, lo)
    yield
    atot = jnp.sum(lw, axis=0, keepdims=True)
    g_end = jnp.exp(atot - a)
    g_inv = jnp.exp(-a)
    kt = kk * jnp.exp(a - lw)
    rt = r * jnp.exp(a)
    lhs = jnp.concatenate([kt, rt], axis=0).astype(BF16)
    kbar = kd * g_inv
    bbar = kb * g_inv
    rhs = jnp.concatenate([_stack_heads(kbar, m0), _stack_heads(bbar, m0)], axis=0)
    g = _dot_nt(lhs, rhs)
    sh = _dot_nt(lhs, s2.astype(BF16))
    yield
    a_k = jnp.where(strict2, g[:c, :2 * c], 0.0)
    a_b = jnp.where(strict2, g[:c, 2 * c:], 0.0)
    a_rk = jnp.where(incl2, g[c:, :2 * c], 0.0)
    a_rb = jnp.where(incl2, g[c:, 2 * c:], 0.0)
    vs = _stack_heads(v, m0)
    x = sh[:c] + _dot(a_k.astype(BF16), vs)
    o_part = sh[c:] + _dot(a_rk.astype(BF16), vs)
    yield
    n = -a_b
    levels = int(math.log2(c))
    for lvl in range(levels):
        nb16 = n.astype(BF16)
        x = x + _dot(nb16, _stack_heads(x, m0))
        if lvl + 1 < levels:
            nbd = jnp.concatenate([jnp.where(cm0, n, 0.0), jnp.where(cm0, 0.0, n)], axis=0).astype(BF16)
            n = _dot(nb16, nbd)
        yield
    u = x
    o = o_part - _dot(a_rb.astype(BF16), _stack_heads(u, m0))
    w_val = jnp.concatenate([v, u], axis=0).astype(BF16)
    w_key = jnp.concatenate([kd * g_end, -(kb * g_end)], axis=0).astype(BF16)
    upd = lax.dot_general(w_val, w_key, (((0,), (0,)), ((), ())), preferred_element_type=F32)
    yield
    s_new = s2 * jnp.exp(atot) + jnp.where(bd, upd, 0.0)
    return o, s_new


def _scan_kernel(lw_ref, r_ref, v_ref, kk_ref, kd_ref, kb_ref, s0_ref, o_ref, sout_ref, s_ref, *, nchunk, npairs):
    d = pl.program_id(1)
    i = pl.program_id(2)

    @pl.when(i == 0)
    def _():
        s_ref[...] = s0_ref[...]

    c = SCAN_CHUNK
    sign = 1 - 2 * d
    tri = ((lax.broadcasted_iota(jnp.int32, (c, c), 1) - lax.broadcasted_iota(jnp.int32, (c, c), 0)) * sign
           <= 0).astype(BF16)
    diff2 = ((lax.broadcasted_iota(jnp.int32, (c, 2 * c), 1) & (c - 1))
             - lax.broadcasted_iota(jnp.int32, (c, 2 * c), 0)) * sign
    strict2 = diff2 < 0
    incl2 = diff2 <= 0
    m0 = _lane_mask((c, LANES))
    cm0 = lax.broadcasted_iota(jnp.int32, (c, 2 * c), 1) < c
    bd = (lax.broadcasted_iota(jnp.int32, (LANES, LANES), 0) // HEAD_DIM
          == lax.broadcasted_iota(jnp.int32, (LANES, LANES), 1) // HEAD_DIM)
    lanes = [slice(p * LANES, (p + 1) * LANES) for p in range(npairs)]
    results = _run_lockstep([
        _scan_pair(lw_ref[:, sl], r_ref[:, sl], v_ref[:, sl], kk_ref[:, sl], kd_ref[:, sl], kb_ref[:, sl],
                   s_ref[p], tri, strict2, incl2, m0, cm0, bd)
        for p, sl in enumerate(lanes)])
    for p, (o, s_new) in enumerate(results):
        o_ref[:, lanes[p]] = o
        s_ref[p] = s_new

    @pl.when(i == nchunk - 1)
    def _():
        sout_ref[...] = s_ref[...]


def _rwkv_scan(ops, s0):
    _, b, l, dh = ops.shape
    c = SCAN_CHUNK
    assert l % c == 0
    nchunk = l // c
    npairs = dh // LANES
    tb = lambda d, i: i + d * (nchunk - 1 - 2 * i)
    shared = lambda idx: pl.BlockSpec((None, None, c, dh), lambda bb, d, i: (idx, bb, tb(d, i), 0))
    perdir = lambda idx: pl.BlockSpec((None, None, c, dh), lambda bb, d, i: (idx + d, bb, tb(d, i), 0))
    sspec = pl.BlockSpec((None, None, npairs, LANES, LANES), lambda bb, d, i: (bb, d, 0, 0, 0))
    return pl.pallas_call(
        functools.partial(_scan_kernel, nchunk=nchunk, npairs=npairs),
        grid=(b, 2, nchunk),
        in_specs=[perdir(3), shared(0), shared(1), shared(2), perdir(5), perdir(7), sspec],
        out_specs=[pl.BlockSpec((None, None, c, dh), lambda bb, d, i: (d, bb, tb(d, i), 0)), sspec],
        out_shape=[jax.ShapeDtypeStruct((2, b, l, dh), F32),
                   jax.ShapeDtypeStruct((b, 2, npairs, LANES, LANES), F32)],
        scratch_shapes=[pltpu.VMEM((npairs, LANES, LANES), F32)],
        compiler_params=_params("parallel", "arbitrary", "arbitrary"),
        name="rwkv_scan",
    )(ops, ops, ops, ops, ops, ops, s0)


def _rwkv_post_kernel(of_ref, ob_ref, bonus_ref, gb_ref, g_ref, b_ref, o_ref):
    ones = _head_ones()
    o = of_ref[...] + ob_ref[...]
    mu = _head_sum(o, ones) * (1.0 / HEAD_DIM)
    oc = o - mu
    var = _head_sum(oc * oc, ones) * (1.0 / HEAD_DIM)
    o = oc * lax.rsqrt(var + GN_EPS) * g_ref[...] + b_ref[...] + bonus_ref[...]
    o_ref[...] = (o * _silu(gb_ref[...])).astype(BF16)


def _rwkv_post(o2, bonus, z, dh, gn_g, gn_b):
    _, b, l, _ = o2.shape
    tl = _tile(l, 512, SUBLANES)
    nb = dh // LANES
    gate_blk = 7 * nb
    return pl.pallas_call(
        _rwkv_post_kernel,
        grid=(b, l // tl, nb),
        in_specs=[pl.BlockSpec((None, None, tl, LANES), lambda bb, i, j: (0, bb, i, j)),
                  pl.BlockSpec((None, None, tl, LANES), lambda bb, i, j: (1, bb, i, j)),
                  pl.BlockSpec((None, tl, LANES), lambda bb, i, j: (bb, i, j)),
                  pl.BlockSpec((None, tl, LANES), lambda bb, i, j: (bb, i, gate_blk + j)),
                  pl.BlockSpec((1, LANES), lambda bb, i, j: (0, j)),
                  pl.BlockSpec((1, LANES), lambda bb, i, j: (0, j))],
        out_specs=pl.BlockSpec((None, tl, LANES), lambda bb, i, j: (bb, i, j)),
        out_shape=jax.ShapeDtypeStruct((b, l, dh), BF16),
        compiler_params=_params("parallel", "parallel", "parallel"),
        name="rwkv_post",
    )(o2, o2, bonus, z, gn_g.reshape(1, dh), gn_b.reshape(1, dh))


def _rwkv_heads(z, dh, pe, s0):
    b, l, _ = z.shape
    nh = dh // HEAD_DIM
    npairs = dh // LANES
    ops, bonus = _rwkv_prep(z, dh, pe["conv_w"], pe["w0"], pe["wup"], pe["a0"], pe["aup"], pe["k_k"], pe["k_a"], pe["r_k"])
    if s0 is None:
        st = jnp.zeros((b, 2, npairs, LANES, LANES), F32)
    else:
        s = s0.astype(F32).reshape(b, 2, npairs, 2, HEAD_DIM, HEAD_DIM)
        zero = jnp.zeros_like(s[:, :, :, 0])
        st = jnp.concatenate([jnp.concatenate([s[:, :, :, 0], zero], axis=-1),
                              jnp.concatenate([zero, s[:, :, :, 1]], axis=-1)], axis=-2)
    o2, s_fin = _rwkv_scan(ops, st)
    s_fin = jnp.stack([s_fin[..., :HEAD_DIM, :HEAD_DIM], s_fin[..., HEAD_DIM:, HEAD_DIM:]], axis=3)
    s_fin = s_fin.reshape(b, 2, nh, HEAD_DIM, HEAD_DIM)
    mixed = _rwkv_post(o2, bonus, z, dh, pe["gn_g"], pe["gn_b"])
    return mixed, s_fin


def _even_layer(x, sc, sh, gt, ln_g, ln_b, alpha, pe, kv_ctx, s0):
    dh = pe["dh"]
    z = _modmm(x, sc, sh, pe["w_in"])
    if kv_ctx is None:
        mixed_a = _ctx_attn(z, dh)
    else:
        mixed_a = _na_attn(z, dh, kv_ctx[0], kv_ctx[1], pe["bias"])
    mixed_b, s_fin = _rwkv_heads(z, dh, pe, s0)
    out = _outln([mixed_a, mixed_b], [pe["w_out"][:dh], pe["w_out"][dh:]], x, gt, ln_g, ln_b, alpha)
    return out, z, s_fin


def _odd_layer(x, sc, sh, gt, ln_g, ln_b, alpha, po, spec, mats):
    c = po["c"]
    z = _modmm(x, sc, sh, po["w_in"])
    x0, uu, uub = _hy_conv(z, po["conv_w"], po["conv_b"], c)
    yf = _dft_fwd(mats[0], uub, spec[0], spec[1])
    mixed = _dft_inv(mats[1], yf, uu, x0, z, po["skip"])
    return _outln([mixed], [po["w_out"]], x, gt, ln_g, ln_b, alpha)


def kernel(x_prompt, x_sample, c, cache_a_k, cache_a_v, state_b, c_ctx, w_ada, b_ada, ln_g, ln_b,
           w_in_even, w_out_even, a_rpb, b_conv_w, b_w0, b_wup, b_a0, b_aup, b_kk, b_ka, b_rk,
           b_gn_g, b_gn_b, w_in_odd, w_out_odd, h_conv_w, h_conv_b, h_ffn_w1, h_ffn_b1, h_ffn_w2,
           h_ffn_b2, h_ffn_w3, h_ffn_b3, h_freq, h_ffn_wout, h_skip):
    depth, d, _ = w_ada.shape
    bx, lx, _ = x_prompt.shape
    by, ly, _ = x_sample.shape
    alpha = (2 * depth) ** 0.25
    dh = b_kk.shape[1]
    nh = dh // HEAD_DIM
    p_len = cache_a_k.shape[2]
    c_hy = h_skip.shape[1]

    rows = 1 + by
    rpad = -(-rows // SUBLANES) * SUBLANES
    cond = jnp.pad(jnp.concatenate([c_ctx[None, :], c], axis=0), ((0, rpad - rows), (0, 0)))
    mods = _ada_mod(cond, w_ada, b_ada)

    x = x_prompt
    y = x_sample
    mats_x = _dft_mats(lx) if depth > 1 else None
    mats_y = _dft_mats(ly) if depth > 1 else None
    new_k, new_v, new_s = [], [], []
    for l in range(depth):
        m = mods[l]
        sh_x, sc_x, gt_x = (jnp.broadcast_to(m[0:1, k * d:(k + 1) * d][None], (bx, 1, d)) for k in range(3))
        sh_y, sc_y, gt_y = (m[1:rows, k * d:(k + 1) * d][:, None, :] for k in range(3))
        if l % 2 == 0:
            e = l // 2
            pe = dict(dh=dh, w_in=w_in_even[e].astype(BF16), w_out=w_out_even[e].astype(BF16),
                      bias=_na_bias_table(a_rpb[e]), conv_w=b_conv_w[e], w0=b_w0[e], wup=b_wup[e], a0=b_a0[e],
                      aup=b_aup[e], k_k=b_kk[e], k_a=b_ka[e], r_k=b_rk[e], gn_g=b_gn_g[e], gn_b=b_gn_b[e])
            x, zx, s_fin = _even_layer(x, sc_x, sh_x, gt_x, ln_g[l], ln_b[l], alpha, pe, None, None)
            new_k.append(zx[..., dh:2 * dh].reshape(bx, lx, nh, HEAD_DIM))
            new_v.append(zx[..., 2 * dh:3 * dh].reshape(bx, lx, nh, HEAD_DIM))
            new_s.append(s_fin)
            kv = (cache_a_k[:, e].reshape(by, p_len, dh), cache_a_v[:, e].reshape(by, p_len, dh))
            y, _, _ = _even_layer(y, sc_y, sh_y, gt_y, ln_g[l], ln_b[l], alpha, pe, kv, state_b[:, e])
        else:
            o = l // 2
            po = dict(c=c_hy, w_in=w_in_odd[o].astype(BF16), w_out=w_out_odd[o].astype(BF16),
                      conv_w=h_conv_w[o], conv_b=h_conv_b[o], skip=h_skip[o])
            fargs = (h_ffn_w1[o], h_ffn_b1[o], h_ffn_w2[o], h_ffn_b2[o], h_ffn_w3[o], h_ffn_b3[o],
                     h_freq[o], h_ffn_wout[o])
            spec_x = _hy_spectrum(lx, c_hy, mats_x[0], *fargs)
            spec_y = _hy_spectrum(ly, c_hy, mats_y[0], *fargs)
            x = _odd_layer(x, sc_x, sh_x, gt_x, ln_g[l], ln_b[l], alpha, po, spec_x, mats_x)
            y = _odd_layer(y, sc_y, sh_y, gt_y, ln_g[l], ln_b[l], alpha, po, spec_y, mats_y)
    return (x, y, jnp.stack(new_k, axis=1), jnp.stack(new_v, axis=1), jnp.stack(new_s, axis=1))
```

```python
import functools
import math

import jax
import jax.numpy as jnp
from jax import lax
from jax.experimental import pallas as pl
from jax.experimental.pallas import tpu as pltpu

F32 = jnp.float32
BF16 = jnp.bfloat16

HEAD_DIM = 64
LORA = 64
GRID_W = 64
WIN_H = 8
WIN_W = 16
HY_TARGET = 1e-2
HY_FAST = 0.3
HY_SLOW = 1.5
LN_EPS = 1e-5
GN_EPS = 64e-5
NEG = -1e30

LANES = 128
SUBLANES = 8
VMEM_LIMIT = 56 * 1024 * 1024


def _tile(n, target, mult):
    if n <= target:
        return n
    t = (target // mult) * mult
    while t >= mult:
        if n % t == 0:
            return t
        t -= mult
    return n


def _params(*sem):
    return pltpu.CompilerParams(dimension_semantics=sem, vmem_limit_bytes=VMEM_LIMIT)


def _silu(x):
    return x * jax.nn.sigmoid(x)


def _dot(a, b):
    return jnp.dot(a, b, preferred_element_type=F32)


def _dot_nt(a, b):
    return lax.dot_general(a, b, (((1,), (1,)), ((), ())), preferred_element_type=F32)


def _run_lockstep(gens):
    results = [None] * len(gens)
    active = list(range(len(gens)))
    while active:
        for p in list(active):
            try:
                next(gens[p])
            except StopIteration as done:
                results[p] = done.value
                active.remove(p)
    return results


def _head_ones():
    r = lax.broadcasted_iota(jnp.int32, (LANES, LANES), 0) // HEAD_DIM
    c = lax.broadcasted_iota(jnp.int32, (LANES, LANES), 1) // HEAD_DIM
    return (r == c).astype(BF16)


def _head_sum(x, ones):
    hi = x.astype(BF16)
    r1 = x - hi.astype(F32)
    mid = r1.astype(BF16)
    lo = (r1 - mid.astype(F32)).astype(BF16)
    return _dot(hi, ones) + _dot(mid, ones) + _dot(lo, ones)


def _ada_kernel(c_ref, w_ref, b_ref, o_ref):
    c = c_ref[...]
    o_ref[...] = _dot(_silu(c).astype(BF16), w_ref[...].astype(BF16)) + b_ref[...]


def _ada_mod(cond, w_ada, b_ada):
    depth, d, n = w_ada.shape
    r = cond.shape[0]
    tn = _tile(n, 768, LANES)
    return pl.pallas_call(
        _ada_kernel,
        grid=(depth, n // tn),
        in_specs=[pl.BlockSpec((r, d), lambda l, j: (0, 0)),
                  pl.BlockSpec((None, d, tn), lambda l, j: (l, 0, j)),
                  pl.BlockSpec((None, 1, tn), lambda l, j: (l, 0, j))],
        out_specs=pl.BlockSpec((None, r, tn), lambda l, j: (l, 0, j)),
        out_shape=jax.ShapeDtypeStruct((depth, r, n), F32),
        compiler_params=_params("parallel", "parallel"),
        name="ada_mod",
    )(cond, w_ada, b_ada.reshape(depth, 1, n))


def _modmm_kernel(*refs, nx, plane_out):
    x_refs = refs[:nx]
    sc_ref, sh_ref, w_ref, o_ref, h_ref = refs[nx:]
    rows = x_refs[0].shape[0]

    @pl.when(pl.program_id(2) == 0)
    def _():
        for r in range(nx):
            h_ref[r * rows:(r + 1) * rows, :] = (x_refs[r][...] * (1.0 + sc_ref[...]) + sh_ref[...]).astype(BF16)

    out = _dot(h_ref[...], w_ref[...])
    if plane_out:
        for r in range(nx):
            o_ref[r] = out[r * rows:(r + 1) * rows]
    else:
        o_ref[...] = out


def _modmm(x, sc, sh, w, planes=None):
    b, l, d = x.shape
    n = w.shape[1]
    tn = _tile(n, 1024, LANES)
    if planes is None:
        rows = _tile(l, 1024, SUBLANES)
        nx = 1
        grid = (b, l // rows, n // tn)
        xs = [x]
        x_specs = [pl.BlockSpec((None, rows, d), lambda bb, i, j: (bb, i, 0))]
        out_spec = pl.BlockSpec((None, rows, tn), lambda bb, i, j: (bb, i, j))
        out_shape = (b, l, n)
    else:
        rows = l // planes
        nx = max(1, min(planes, 1024 // rows))
        assert l % planes == 0 and planes % nx == 0
        grid = (b, planes // nx, n // tn)
        xs = [x.reshape(b, rows, planes * d)] * nx
        x_specs = [pl.BlockSpec((None, rows, d), lambda bb, i, j, r=r: (bb, 0, i * nx + r)) for r in range(nx)]
        out_spec = pl.BlockSpec((None, nx, rows, tn), lambda bb, i, j: (bb, i, 0, j))
        out_shape = (b, planes, rows, n)
    return pl.pallas_call(
        functools.partial(_modmm_kernel, nx=nx, plane_out=planes is not None),
        grid=grid,
        in_specs=x_specs + [pl.BlockSpec((None, 1, d), lambda bb, i, j: (bb, 0, 0)),
                            pl.BlockSpec((None, 1, d), lambda bb, i, j: (bb, 0, 0)),
                            pl.BlockSpec((d, tn), lambda bb, i, j: (0, j))],
        out_specs=out_spec,
        out_shape=jax.ShapeDtypeStruct(out_shape, F32),
        scratch_shapes=[pltpu.VMEM((nx * rows, d), BF16)],
        compiler_params=_params("parallel", "parallel", "arbitrary"),
        name="mod_in_proj",
    )(*xs, sc, sh, w)


def _outln_kernel(*refs, nparts, alpha):
    m_refs = refs[:nparts]
    w_refs = refs[nparts:2 * nparts]
    x_ref, gt_ref, g_ref, b_ref, o_ref = refs[2 * nparts:]
    acc = _dot(m_refs[0][...], w_refs[0][...])
    for p in range(1, nparts):
        acc = acc + _dot(m_refs[p][...], w_refs[p][...])
    y = alpha * x_ref[...] + gt_ref[...] * acc
    mu = jnp.mean(y, axis=-1, keepdims=True)
    yc = y - mu
    var = jnp.mean(yc * yc, axis=-1, keepdims=True)
    o_ref[...] = yc * lax.rsqrt(var + LN_EPS) * g_ref[...] + b_ref[...]


def _outln(parts, ws, x, gt, ln_g, ln_b, alpha):
    b, l, d = x.shape
    tl = _tile(l, 512, SUBLANES)
    nparts = len(parts)
    in_specs = [pl.BlockSpec((None, tl, p.shape[2]), lambda bb, i: (bb, i, 0)) for p in parts]
    in_specs += [pl.BlockSpec(w.shape, lambda bb, i: (0, 0)) for w in ws]
    in_specs += [pl.BlockSpec((None, tl, d), lambda bb, i: (bb, i, 0)),
                 pl.BlockSpec((None, 1, d), lambda bb, i: (bb, 0, 0)),
                 pl.BlockSpec((1, d), lambda bb, i: (0, 0)),
                 pl.BlockSpec((1, d), lambda bb, i: (0, 0))]
    return pl.pallas_call(
        functools.partial(_outln_kernel, nparts=nparts, alpha=alpha),
        grid=(b, l // tl),
        in_specs=in_specs,
        out_specs=pl.BlockSpec((None, tl, d), lambda bb, i: (bb, i, 0)),
        out_shape=jax.ShapeDtypeStruct((b, l, d), F32),
        compiler_params=_params("parallel", "parallel"),
        name="out_proj_ln",
    )(*parts, *ws, x, gt, ln_g.reshape(1, d), ln_b.reshape(1, d))


def _conv_specs(tl, tc, l, colblk):
    g = tl // SUBLANES
    last = l // SUBLANES - 1
    return [
        pl.BlockSpec((None, tl, tc), lambda bb, i, j: (bb, i, colblk + j)),
        pl.BlockSpec((None, SUBLANES, tc), lambda bb, i, j: (bb, jnp.maximum(i * g - 1, 0), colblk + j)),
        pl.BlockSpec((None, SUBLANES, tc), lambda bb, i, j: (bb, jnp.minimum((i + 1) * g, last), colblk + j)),
    ]


def _conv3(x_ref, p_ref, n_ref, w, first, last):
    x = x_ref[...]
    tl = x.shape[0]
    row = lax.broadcasted_iota(jnp.int32, x.shape, 0)
    prow = jnp.where(first, 0.0, p_ref[SUBLANES - 1:SUBLANES, :])
    nrow = jnp.where(last, 0.0, n_ref[0:1, :])
    xm = jnp.where(row == 0, prow, pltpu.roll(x, 1, 0))
    xq = jnp.where(row == tl - 1, nrow, pltpu.roll(x, tl - 1, 0))
    return xm * w[0:1, :] + x * w[1:2, :] + xq * w[2:3, :]


HY_PLANES = 16


def _plane_conv3(z_ref, w, bias):
    p, rows, _ = z_ref.shape
    row = lax.broadcasted_iota(jnp.int32, z_ref.shape[1:], 0)
    before = jnp.where(row == 0, 0.0, pltpu.roll(z_ref[p - 1], 1, 0))
    after = jnp.where(row == rows - 1, 0.0, pltpu.roll(z_ref[0], rows - 1, 0))
    out = []
    for t2 in range(p):
        prev = z_ref[t2 - 1] if t2 > 0 else before
        nxt = z_ref[t2 + 1] if t2 < p - 1 else after
        out.append(prev * w[0:1, :] + z_ref[t2] * w[1:2, :] + nxt * w[2:3, :] + bias)
    return out


def _hy_conv_kernel(z0_ref, z1_ref, z2_ref, w0_ref, w1_ref, w2_ref, b0_ref, b1_ref, b2_ref,
                    x0o_ref, uu_ref, uub_ref):
    x0 = _plane_conv3(z0_ref, w0_ref[...], b0_ref[...])
    x1 = _plane_conv3(z1_ref, w1_ref[...], b1_ref[...])
    v = _plane_conv3(z2_ref, w2_ref[...], b2_ref[...])
    for t2 in range(len(x0)):
        uu = x1[t2] * v[t2]
        x0o_ref[t2] = x0[t2]
        uu_ref[t2] = uu
        uub_ref[t2] = uu.astype(BF16)


def _hy_conv(z, conv_w, conv_b, c):
    b, p, rows, _ = z.shape
    tc = _tile(c, 256, LANES)
    nc = c // tc
    in_specs = [pl.BlockSpec((None, p, rows, tc), lambda bb, j, k=k: (bb, 0, 0, k * nc + j)) for k in range(3)]
    in_specs += [pl.BlockSpec((3, tc), lambda bb, j, k=k: (0, k * nc + j)) for k in range(3)]
    in_specs += [pl.BlockSpec((1, tc), lambda bb, j, k=k: (0, k * nc + j)) for k in range(3)]
    ospec = pl.BlockSpec((None, p, rows, tc), lambda bb, j: (bb, 0, 0, j))
    cb = conv_b.reshape(1, 3 * c)
    return pl.pallas_call(
        _hy_conv_kernel,
        grid=(b, nc),
        in_specs=in_specs,
        out_specs=[ospec, ospec, ospec],
        out_shape=[jax.ShapeDtypeStruct((b, p, rows, c), F32), jax.ShapeDtypeStruct((b, p, rows, c), F32),
                   jax.ShapeDtypeStruct((b, p, rows, c), BF16)],
        compiler_params=_params("parallel", "parallel"),
        name="hyena_gate_conv",
    )(z, z, z, conv_w, conv_w, conv_w, cb, cb, cb)


def _hy_filter_kernel(feats_ref, t_ref, fw1_ref, fb1_ref, fw2_ref, fb2_ref, fw3_ref, fb3_ref, freq_ref,
                      wf_ref, wb_ref, dl_ref, o_ref):
    freq = freq_ref[...]
    hdn = jnp.sin(freq[0:1, :] * (_dot(feats_ref[...].astype(BF16), fw1_ref[...].astype(BF16)) + fb1_ref[...]))
    hdn = jnp.sin(freq[1:2, :] * (_dot(hdn.astype(BF16), fw2_ref[...].astype(BF16)) + fb2_ref[...]))
    hdn = jnp.sin(freq[2:3, :] * (_dot(hdn.astype(BF16), fw3_ref[...].astype(BF16)) + fb3_ref[...]))
    hb16 = hdn.astype(BF16)
    window = jnp.exp(-t_ref[...] * dl_ref[...])
    h_f = _dot(hb16, wf_ref[...].astype(BF16)) * window
    h_b = _dot(hb16, wb_ref[...].astype(BF16)) * window
    h_b = h_b * (t_ref[...] > 0.0).astype(F32)
    norm = jnp.sum(jnp.abs(h_f), axis=0, keepdims=True) + jnp.sum(jnp.abs(h_b), axis=0, keepdims=True)
    inv = 1.0 / norm
    o_ref[0] = (h_f * inv).astype(BF16)
    o_ref[1] = (h_b * inv).astype(BF16)


def _hy_filter(l, c, fw1, fb1, fw2, fb2, fw3, fb3, freq, fwout):
    emb, order = fw1.shape
    bands_n = (emb - 1) // 2
    t = jnp.linspace(0.0, 1.0, l, dtype=F32)[:, None]
    ang = 2.0 * math.pi * jnp.arange(l, dtype=F32)[:, None] / l
    bands = jnp.linspace(1e-4, bands_n - 1, bands_n, dtype=F32)
    feats = jnp.concatenate([t, jnp.cos(ang * bands), -jnp.sin(ang * bands)], axis=-1)
    feats = jnp.pad(feats, ((0, 0), (0, LANES - emb)))
    plane_major = lambda a: a.reshape(l // HY_PLANES, HY_PLANES, a.shape[1]).transpose(1, 0, 2).reshape(l, a.shape[1])
    feats, t = plane_major(feats), plane_major(t)
    fw1p = jnp.pad(fw1, ((0, LANES - emb), (0, 0)))
    deltas = jnp.abs(jnp.linspace(math.log(HY_TARGET) / HY_SLOW, math.log(HY_TARGET) / HY_FAST, c, dtype=F32))
    tc = _tile(c, 256, LANES)
    nc = c // tc
    full = lambda shape: pl.BlockSpec(shape, lambda j: tuple(0 for _ in shape))
    return pl.pallas_call(
        _hy_filter_kernel,
        grid=(nc,),
        in_specs=[full((l, LANES)), full((l, 1)), full((LANES, order)), full((1, order)),
                  full((order, order)), full((1, order)), full((order, order)), full((1, order)),
                  full((3, order)),
                  pl.BlockSpec((order, tc), lambda j: (0, j)),
                  pl.BlockSpec((order, tc), lambda j: (0, nc + j)),
                  pl.BlockSpec((1, tc), lambda j: (0, j))],
        out_specs=pl.BlockSpec((2, l, tc), lambda j: (0, 0, j)),
        out_shape=jax.ShapeDtypeStruct((2, l, c), BF16),
        compiler_params=_params("parallel"),
        name="hyena_filter",
    )(feats, t, fw1p, fb1.reshape(1, order), fw2, fb2.reshape(1, order), fw3, fb3.reshape(1, order),
      freq, fwout, fwout, deltas.reshape(1, c))


def _hy_mats(l):
    p = HY_PLANES
    rows = l // p
    n = 2 * l
    kq = jnp.arange(rows, dtype=jnp.int32)
    t = p * jnp.arange(rows, dtype=jnp.int32)[None, None, :] + jnp.arange(p, dtype=jnp.int32)[:, None, None]
    phase = ((2 * kq[None, :, None] + 1) * t) % (2 * n)
    ang = phase.astype(F32) * (math.pi / n)
    cos, sin = jnp.cos(ang), jnp.sin(ang)
    fwd = jnp.concatenate([cos, -sin], axis=1)
    inv = (2.0 / n) * jnp.concatenate([cos.transpose(0, 2, 1), -sin.transpose(0, 2, 1)], axis=2)
    return fwd.astype(BF16), inv.astype(BF16)


def _bit_reverse(i, n):
    return int(format(i, "0%db" % int(math.log2(n)))[::-1], 2)


def _plane_butterflies(s_ref, sign, dif):
    p = s_ref.shape[1]
    sizes = [p >> i for i in range(int(math.log2(p)))]
    if not dif:
        sizes = sizes[::-1]
    for size in sizes:
        half = size // 2
        for start in range(0, p, size):
            for k in range(half):
                ang = sign * 2.0 * math.pi * k / size
                wr, wi = math.cos(ang), math.sin(ang)
                i0, i1 = start + k, start + k + half
                ur, ui, vr, vi = s_ref[0, i0], s_ref[1, i0], s_ref[0, i1], s_ref[1, i1]

                def twiddle(xr, xi):
                    if k == 0:
                        return xr, xi
                    if 4 * k == size:
                        return (-xi, xr) if sign > 0 else (xi, -xr)
                    return xr * wr - xi * wi, xr * wi + xi * wr

                if dif:
                    s_ref[0, i0] = ur + vr
                    s_ref[1, i0] = ui + vi
                    tr, ti = twiddle(ur - vr, ui - vi)
                    s_ref[0, i1] = tr
                    s_ref[1, i1] = ti
                else:
                    tr, ti = twiddle(vr, vi)
                    s_ref[0, i0] = ur + tr
                    s_ref[1, i0] = ui + ti
                    s_ref[0, i1] = ur - tr
                    s_ref[1, i1] = ui - ti


def _plane_spectrum(x_ref, f_ref, s_ref):
    p, rows, _ = x_ref.shape
    for t2 in range(p):
        y = _dot(f_ref[t2], x_ref[t2])
        s_ref[0, t2] = y[:rows]
        s_ref[1, t2] = y[rows:]
    _plane_butterflies(s_ref, -1, dif=True)


def _hy_spec_kernel(h_ref, f_ref, o_ref, s_ref):
    p = s_ref.shape[1]
    _plane_spectrum(h_ref.at[0], f_ref, s_ref)
    for i in range(p):
        o_ref[0, i] = s_ref[0, i]
        o_ref[1, i] = s_ref[1, i]
    _plane_spectrum(h_ref.at[1], f_ref, s_ref)
    for i in range(p):
        o_ref[0, i] = o_ref[0, i] + s_ref[0, i]
        o_ref[1, i] = o_ref[1, i] - s_ref[1, i]


def _hy_spectrum(l, c, fwd, fw1, fb1, fw2, fb2, fw3, fb3, freq, fwout):
    p = HY_PLANES
    rows = l // p
    hfb = _hy_filter(l, c, fw1, fb1, fw2, fb2, fw3, fb3, freq, fwout).reshape(2, p, rows, c)
    tn = LANES
    return pl.pallas_call(
        _hy_spec_kernel,
        grid=(c // tn,),
        in_specs=[pl.BlockSpec((2, p, rows, tn), lambda j: (0, 0, 0, j)),
                  pl.BlockSpec((p, 2 * rows, rows), lambda j: (0, 0, 0))],
        out_specs=pl.BlockSpec((2, p, rows, tn), lambda j: (0, 0, 0, j)),
        out_shape=jax.ShapeDtypeStruct((2, p, rows, c), F32),
        scratch_shapes=[pltpu.VMEM((2, p, rows, tn), F32)],
        compiler_params=_params("parallel"),
        name="hyena_filter_spectrum",
    )(hfb, fwd)


def _hy_fftconv_kernel(x_ref, f_ref, g_ref, h_ref, uu_ref, x0_ref, gz_ref, skip_ref, o_ref, s_ref):
    p, rows, _ = x_ref.shape
    _plane_spectrum(x_ref, f_ref, s_ref)
    for i in range(p):
        re, im, hre, him = s_ref[0, i], s_ref[1, i], h_ref[0, i], h_ref[1, i]
        s_ref[0, i] = re * hre - im * him
        s_ref[1, i] = re * him + im * hre
    _plane_butterflies(s_ref, 1, dif=False)
    skip = skip_ref[...]
    for t2 in range(p):
        spec = jnp.concatenate([s_ref[0, t2], s_ref[1, t2]], axis=0).astype(BF16)
        y = _dot(g_ref[t2], spec) + uu_ref[t2] * skip
        o_ref[t2] = (_silu(gz_ref[t2]) * (x0_ref[t2] * y)).astype(BF16)


def _hy_fftconv(mats, spec, uub, uu, x0, z, skip):
    b, p, rows, c = uu.shape
    tn = LANES
    nc = c // tn
    tile = pl.BlockSpec((None, p, rows, tn), lambda bb, j: (bb, 0, 0, j))
    return pl.pallas_call(
        _hy_fftconv_kernel,
        grid=(b, nc),
        in_specs=[tile,
                  pl.BlockSpec((p, 2 * rows, rows), lambda bb, j: (0, 0, 0)),
                  pl.BlockSpec((p, rows, 2 * rows), lambda bb, j: (0, 0, 0)),
                  pl.BlockSpec((2, p, rows, tn), lambda bb, j: (0, 0, 0, j)),
                  tile, tile,
                  pl.BlockSpec((None, p, rows, tn), lambda bb, j: (bb, 0, 0, 3 * nc + j)),
                  pl.BlockSpec((1, tn), lambda bb, j: (0, j))],
        out_specs=tile,
        out_shape=jax.ShapeDtypeStruct((b, p, rows, c), BF16),
        scratch_shapes=[pltpu.VMEM((2, p, rows, tn), F32)],
        compiler_params=_params("parallel", "parallel"),
        name="hyena_fftconv",
    )(uub, mats[0], mats[1], spec, uu, x0, z, skip.reshape(1, c))


def _lane_mask(shape):
    return lax.broadcasted_iota(jnp.int32, shape, 1) < HEAD_DIM


def _ctx_attn_kernel(q_ref, k_ref, v_ref, ga_ref, o_ref):
    q = q_ref[...]
    k = k_ref[...].astype(BF16)
    v = v_ref[...].astype(BF16)
    m0 = _lane_mask(q.shape)
    scale = HEAD_DIM ** -0.5
    outs = []
    for h in range(2):
        qh = jnp.where(m0 if h == 0 else jnp.logical_not(m0), q, 0.0).astype(BF16)
        s = _dot_nt(qh, k) * scale
        p = jnp.exp(s - jnp.max(s, axis=-1, keepdims=True))
        den = jnp.sum(p, axis=-1, keepdims=True)
        outs.append(_dot(p.astype(BF16), v) / den)
    o = jnp.where(m0, outs[0], outs[1])
    o_ref[...] = (o * _silu(ga_ref[...])).astype(BF16)


def _ctx_attn(z, dh):
    b, l, _ = z.shape
    nb = dh // LANES
    spec = lambda g: pl.BlockSpec((None, l, LANES), lambda bb, hp, g=g: (bb, 0, g * nb + hp))
    return pl.pallas_call(
        _ctx_attn_kernel,
        grid=(b, nb),
        in_specs=[spec(0), spec(1), spec(2), spec(3)],
        out_specs=pl.BlockSpec((None, l, LANES), lambda bb, hp: (bb, 0, hp)),
        out_shape=jax.ShapeDtypeStruct((b, l, dh), BF16),
        compiler_params=_params("parallel", "parallel"),
        name="ctx_attention",
    )(z, z, z, z)


def _na_bias_table(rpb):
    c = jnp.arange(GRID_W)[:, None]
    kc = jnp.arange(GRID_W)[None, :]
    qstart = jnp.clip(c - WIN_W // 2, 0, GRID_W - WIN_W)
    mask = (kc >= qstart) & (kc < qstart + WIN_W)
    cidx = jnp.clip(kc - c, 1 - WIN_W, WIN_W - 1) + WIN_W - 1
    dr = jnp.arange(WIN_H)[:, None] + jnp.arange(WIN_H)[None, :]
    t = rpb[:, dr][..., cidx]
    t = jnp.where(mask, t, NEG)
    nh = rpb.shape[0]
    t = t.transpose(1, 0, 3, 2, 4).reshape(WIN_H, nh // 2, 2 * GRID_W, WIN_H * GRID_W)
    return t.transpose(1, 0, 2, 3)


NA_ROWS_PER_STEP = 4


def _na_kernel(q_ref, k_ref, v_ref, kc_ref, vc_ref, bias_ref, ga_ref, o_ref, kb_ref, vb_ref, *, rows):
    kb_ref[...] = k_ref[...].astype(BF16)
    vb_ref[...] = v_ref[...].astype(BF16)
    kc = kc_ref[...].astype(BF16)
    vc = vc_ref[...].astype(BF16)
    m0 = _lane_mask((GRID_W, LANES))
    scale = HEAD_DIM ** -0.5
    win = WIN_H * GRID_W

    def one_row(r):
        rs = jnp.clip(r - WIN_H // 2, 0, rows - WIN_H)
        d0 = rs - r + WIN_H - 1
        q0 = pl.multiple_of(r * GRID_W, GRID_W)
        k0 = pl.multiple_of(rs * GRID_W, GRID_W)
        q = q_ref[pl.ds(q0, GRID_W), :]
        q2 = jnp.concatenate([jnp.where(m0, q, 0.0), jnp.where(m0, 0.0, q)], axis=0).astype(BF16)
        kw = kb_ref[pl.ds(k0, win), :]
        vw = vb_ref[pl.ds(k0, win), :]
        s_nb = _dot_nt(q2, kw)
        s_cx = _dot_nt(q2, kc)
        yield
        s_nb = s_nb * scale + bias_ref[d0]
        s_cx = s_cx * scale
        m = jnp.maximum(jnp.max(s_nb, axis=-1, keepdims=True), jnp.max(s_cx, axis=-1, keepdims=True))
        p_nb = jnp.exp(s_nb - m)
        p_cx = jnp.exp(s_cx - m)
        den = jnp.sum(p_nb, axis=-1, keepdims=True) + jnp.sum(p_cx, axis=-1, keepdims=True)
        o2 = _dot(p_nb.astype(BF16), vw) + _dot(p_cx.astype(BF16), vc)
        yield
        o2 = o2 / den
        o = jnp.where(m0, o2[:GRID_W], o2[GRID_W:])
        ga = ga_ref[pl.ds(q0, GRID_W), :]
        o_ref[pl.ds(q0, GRID_W), :] = (o * _silu(ga)).astype(BF16)

    def body(i, carry):
        _run_lockstep([one_row(i * NA_ROWS_PER_STEP + u) for u in range(NA_ROWS_PER_STEP)])
        return carry

    lax.fori_loop(0, rows // NA_ROWS_PER_STEP, body, 0)


def _na_attn(z, dh, k_ctx, v_ctx, bias):
    b, t, _ = z.shape
    rows = t // GRID_W
    assert t % GRID_W == 0 and rows >= WIN_H and rows % NA_ROWS_PER_STEP == 0
    p = k_ctx.shape[1]
    nb = dh // LANES
    spec = lambda g: pl.BlockSpec((None, t, LANES), lambda bb, hp, g=g: (bb, 0, g * nb + hp))
    cspec = pl.BlockSpec((None, p, LANES), lambda bb, hp: (bb, 0, hp))
    return pl.pallas_call(
        functools.partial(_na_kernel, rows=rows),
        grid=(b, nb),
        in_specs=[spec(0), spec(1), spec(2), cspec, cspec,
                  pl.BlockSpec((None, WIN_H, 2 * GRID_W, WIN_H * GRID_W), lambda bb, hp: (hp, 0, 0, 0)),
                  spec(3)],
        out_specs=pl.BlockSpec((None, t, LANES), lambda bb, hp: (bb, 0, hp)),
        out_shape=jax.ShapeDtypeStruct((b, t, dh), BF16),
        scratch_shapes=[pltpu.VMEM((t, LANES), BF16), pltpu.VMEM((t, LANES), BF16)],
        compiler_params=_params("parallel", "parallel"),
        name="na_attention",
    )(z, z, z, k_ctx, v_ctx, bias, z)


N_OPS = 9


def _softplus(x):
    return jnp.maximum(x, 0.0) + jnp.log(1.0 + jnp.exp(-jnp.abs(x)))


def _rwkv_prep_kernel(r_ref, rp_ref, rn_ref, k_ref, kp_ref, kn_ref, v_ref, vp_ref, vn_ref,
                      cwr_ref, cwk_ref, cwv_ref, wdn_ref, adn_ref, wup_ref, aup_ref, w0_ref, a0_ref,
                      kk_ref, ka_ref, rk_ref, o_ref, bonus_ref):
    first = pl.program_id(1) == 0
    last = pl.program_id(1) == pl.num_programs(1) - 1
    r = _conv3(r_ref, rp_ref, rn_ref, cwr_ref[...], first, last)
    k = _conv3(k_ref, kp_ref, kn_ref, cwk_ref[...], first, last)
    v = _conv3(v_ref, vp_ref, vn_ref, cwv_ref[...], first, last)
    ones = _head_ones()
    tw = jnp.tanh(wdn_ref[...]).astype(BF16)
    ad = adn_ref[...].astype(BF16)
    k_a = ka_ref[...]
    kk = k * kk_ref[...]
    nrm = jnp.sqrt(_head_sum(kk * kk, ones))
    kk = kk / jnp.maximum(nrm, 1e-12)
    o_ref[0] = r
    o_ref[1] = v
    o_ref[2] = kk
    kd_sum = None
    for d in range(2):
        w_raw = -_softplus(-(w0_ref[d:d + 1, :] + _dot(tw, wup_ref[d].astype(BF16)))) - 0.5
        a = jax.nn.sigmoid(a0_ref[d:d + 1, :] + _dot(ad, aup_ref[d].astype(BF16)))
        kd = k * (1.0 + (a - 1.0) * k_a)
        o_ref[3 + d] = -jnp.exp(w_raw)
        o_ref[5 + d] = kd
        o_ref[7 + d] = kk * a
        kd_sum = kd if kd_sum is None else kd_sum + kd
    bonus_ref[...] = _head_sum(r * kd_sum * rk_ref[...], ones) * v


def _rwkv_prep(z, dh, conv_w, w0, wup, a0, aup, k_k, k_a, r_k):
    b, l, _ = z.shape
    tl = _tile(l, 512, SUBLANES)
    nb = dh // LANES
    base = 4 * nb
    lora_blk = 8 * nb
    in_specs = (_conv_specs(tl, LANES, l, base) + _conv_specs(tl, LANES, l, base + nb)
                + _conv_specs(tl, LANES, l, base + 2 * nb))
    in_specs += [pl.BlockSpec((3, LANES), lambda bb, i, j, g=g: (0, g * nb + j)) for g in range(3)]
    in_specs += [pl.BlockSpec((None, tl, LANES), lambda bb, i, j: (bb, i, lora_blk)),
                 pl.BlockSpec((None, tl, LANES), lambda bb, i, j: (bb, i, lora_blk + 1)),
                 pl.BlockSpec((2, LANES, LANES), lambda bb, i, j: (0, 0, j)),
                 pl.BlockSpec((2, LANES, LANES), lambda bb, i, j: (0, 0, j)),
                 pl.BlockSpec((2, LANES), lambda bb, i, j: (0, j)),
                 pl.BlockSpec((2, LANES), lambda bb, i, j: (0, j)),
                 pl.BlockSpec((1, LANES), lambda bb, i, j: (0, j)),
                 pl.BlockSpec((1, LANES), lambda bb, i, j: (0, j)),
                 pl.BlockSpec((1, LANES), lambda bb, i, j: (0, j))]
    zeros = jnp.zeros((LORA, dh), F32)
    wup_p = jnp.stack([jnp.concatenate([wup[0], zeros]), jnp.concatenate([zeros, wup[1]])])
    aup_p = jnp.stack([jnp.concatenate([aup[0], zeros]), jnp.concatenate([zeros, aup[1]])])
    return pl.pallas_call(
        _rwkv_prep_kernel,
        grid=(b, l // tl, nb),
        in_specs=in_specs,
        out_specs=[pl.BlockSpec((N_OPS, None, tl, LANES), lambda bb, i, j: (0, bb, i, j)),
                   pl.BlockSpec((None, tl, LANES), lambda bb, i, j: (bb, i, j))],
        out_shape=[jax.ShapeDtypeStruct((N_OPS, b, l, dh), F32), jax.ShapeDtypeStruct((b, l, dh), F32)],
        compiler_params=_params("parallel", "parallel", "parallel"),
        name="rwkv_prep",
    )(z, z, z, z, z, z, z, z, z, conv_w, conv_w, conv_w, z, z, wup_p, aup_p, w0, a0,
      k_k.reshape(1, dh), k_a.reshape(1, dh), r_k.reshape(1, dh))


SCAN_CHUNK = 64


def _split3(x):
    hi = x.astype(BF16)
    r1 = x - hi.astype(F32)
    mid = r1.astype(BF16)
    return hi, mid, (r1 - mid.astype(F32)).astype(BF16)


def _stack_heads(x, m0):
    return jnp.concatenate([jnp.where(m0, x, 0.0), jnp.where(m0, 0.0, x)], axis=0).astype(BF16)


def _scan_pair(lw, r, v, kk, kd, kb, s2, tri, strict2, incl2, m0, cm0, bd):
    c = lw.shape[0]
    hi, mid, lo = _split3(lw)
    a = _dot(tri, hi) + _dot(tri, mid) + _dot(tri, lo)
    yield
    atot = jnp.sum(lw, axis=0, keepdims=True)
    g_end = jnp.exp(atot - a)
    g_inv = jnp.exp(-a)
    kt = kk * jnp.exp(a - lw)
    rt = r * jnp.exp(a)
    lhs = jnp.concatenate([kt, rt], axis=0).astype(BF16)
    kbar = kd * g_inv
    bbar = kb * g_inv
    rhs = jnp.concatenate([_stack_heads(kbar, m0), _stack_heads(bbar, m0)], axis=0)
    g = _dot_nt(lhs, rhs)
    sh = _dot_nt(lhs, s2.astype(BF16))
    yield
    a_k = jnp.where(strict2, g[:c, :2 * c], 0.0)
    a_b = jnp.where(strict2, g[:c, 2 * c:], 0.0)
    a_rk = jnp.where(incl2, g[c:, :2 * c], 0.0)
    a_rb = jnp.where(incl2, g[c:, 2 * c:], 0.0)
    vs = _stack_heads(v, m0)
    x = sh[:c] + _dot(a_k.astype(BF16), vs)
    o_part = sh[c:] + _dot(a_rk.astype(BF16), vs)
    yield
    n = -a_b
    levels = int(math.log2(c))
    for lvl in range(levels):
        nb16 = n.astype(BF16)
        x = x + _dot(nb16, _stack_heads(x, m0))
        if lvl + 1 < levels:
            nbd = jnp.concatenate([jnp.where(cm0, n, 0.0), jnp.where(cm0, 0.0, n)], axis=0).astype(BF16)
            n = _dot(nb16, nbd)
        yield
    u = x
    o = o_part - _dot(a_rb.astype(BF16), _stack_heads(u, m0))
    w_val = jnp.concatenate([v, u], axis=0).astype(BF16)
    w_key = jnp.concatenate([kd * g_end, -(kb * g_end)], axis=0).astype(BF16)
    upd = lax.dot_general(w_val, w_key, (((0,), (0,)), ((), ())), preferred_element_type=F32)
    yield
    s_new = s2 * jnp.exp(atot) + jnp.where(bd, upd, 0.0)
    return o, s_new


def _scan_kernel(lw_ref, r_ref, v_ref, kk_ref, kd_ref, kb_ref, s0_ref, o_ref, sout_ref, s_ref, *, nchunk, npairs):
    d = pl.program_id(1)
    i = pl.program_id(2)

    @pl.when(i == 0)
    def _():
        s_ref[...] = s0_ref[...]

    c = SCAN_CHUNK
    sign = 1 - 2 * d
    tri = ((lax.broadcasted_iota(jnp.int32, (c, c), 1) - lax.broadcasted_iota(jnp.int32, (c, c), 0)) * sign
           <= 0).astype(BF16)
    diff2 = ((lax.broadcasted_iota(jnp.int32, (c, 2 * c), 1) & (c - 1))
             - lax.broadcasted_iota(jnp.int32, (c, 2 * c), 0)) * sign
    strict2 = diff2 < 0
    incl2 = diff2 <= 0
    m0 = _lane_mask((c, LANES))
    cm0 = lax.broadcasted_iota(jnp.int32, (c, 2 * c), 1) < c
    bd = (lax.broadcasted_iota(jnp.int32, (LANES, LANES), 0) // HEAD_DIM
          == lax.broadcasted_iota(jnp.int32, (LANES, LANES), 1) // HEAD_DIM)
    lanes = [slice(p * LANES, (p + 1) * LANES) for p in range(npairs)]
    results = _run_lockstep([
        _scan_pair(lw_ref[:, sl], r_ref[:, sl], v_ref[:, sl], kk_ref[:, sl], kd_ref[:, sl], kb_ref[:, sl],
                   s_ref[p], tri, strict2, incl2, m0, cm0, bd)
        for p, sl in enumerate(lanes)])
    for p, (o, s_new) in enumerate(results):
        o_ref[:, lanes[p]] = o
        s_ref[p] = s_new

    @pl.when(i == nchunk - 1)
    def _():
        sout_ref[...] = s_ref[...]


def _rwkv_scan(ops, s0):
    _, b, l, dh = ops.shape
    c = SCAN_CHUNK
    assert l % c == 0
    nchunk = l // c
    npairs = dh // LANES
    tb = lambda d, i: i + d * (nchunk - 1 - 2 * i)
    shared = lambda idx: pl.BlockSpec((None, None, c, dh), lambda bb, d, i: (idx, bb, tb(d, i), 0))
    perdir = lambda idx: pl.BlockSpec((None, None, c, dh), lambda bb, d, i: (idx + d, bb, tb(d, i), 0))
    sspec = pl.BlockSpec((None, None, npairs, LANES, LANES), lambda bb, d, i: (bb, d, 0, 0, 0))
    return pl.pallas_call(
        functools.partial(_scan_kernel, nchunk=nchunk, npairs=npairs),
        grid=(b, 2, nchunk),
        in_specs=[perdir(3), shared(0), shared(1), shared(2), perdir(5), perdir(7), sspec],
        out_specs=[pl.BlockSpec((None, None, c, dh), lambda bb, d, i: (d, bb, tb(d, i), 0)), sspec],
        out_shape=[jax.ShapeDtypeStruct((2, b, l, dh), F32),
                   jax.ShapeDtypeStruct((b, 2, npairs, LANES, LANES), F32)],
        scratch_shapes=[pltpu.VMEM((npairs, LANES, LANES), F32)],
        compiler_params=_params("parallel", "arbitrary", "arbitrary"),
        name="rwkv_scan",
    )(ops, ops, ops, ops, ops, ops, s0)


def _rwkv_post_kernel(of_ref, ob_ref, bonus_ref, gb_ref, g_ref, b_ref, o_ref):
    ones = _head_ones()
    o = of_ref[...] + ob_ref[...]
    mu = _head_sum(o, ones) * (1.0 / HEAD_DIM)
    oc = o - mu
    var = _head_sum(oc * oc, ones) * (1.0 / HEAD_DIM)
    o = oc * lax.rsqrt(var + GN_EPS) * g_ref[...] + b_ref[...] + bonus_ref[...]
    o_ref[...] = (o * _silu(gb_ref[...])).astype(BF16)


def _rwkv_post(o2, bonus, z, dh, gn_g, gn_b):
    _, b, l, _ = o2.shape
    tl = _tile(l, 512, SUBLANES)
    nb = dh // LANES
    gate_blk = 7 * nb
    return pl.pallas_call(
        _rwkv_post_kernel,
        grid=(b, l // tl, nb),
        in_specs=[pl.BlockSpec((None, None, tl, LANES), lambda bb, i, j: (0, bb, i, j)),
                  pl.BlockSpec((None, None, tl, LANES), lambda bb, i, j: (1, bb, i, j)),
                  pl.BlockSpec((None, tl, LANES), lambda bb, i, j: (bb, i, j)),
                  pl.BlockSpec((None, tl, LANES), lambda bb, i, j: (bb, i, gate_blk + j)),
                  pl.BlockSpec((1, LANES), lambda bb, i, j: (0, j)),
                  pl.BlockSpec((1, LANES), lambda bb, i, j: (0, j))],
        out_specs=pl.BlockSpec((None, tl, LANES), lambda bb, i, j: (bb, i, j)),
        out_shape=jax.ShapeDtypeStruct((b, l, dh), BF16),
        compiler_params=_params("parallel", "parallel", "parallel"),
        name="rwkv_post",
    )(o2, o2, bonus, z, gn_g.reshape(1, dh), gn_b.reshape(1, dh))


def _rwkv_heads(z, dh, pe, s0):
    b, l, _ = z.shape
    nh = dh // HEAD_DIM
    npairs = dh // LANES
    ops, bonus = _rwkv_prep(z, dh, pe["conv_w"], pe["w0"], pe["wup"], pe["a0"], pe["aup"], pe["k_k"], pe["k_a"], pe["r_k"])
    if s0 is None:
        st = jnp.zeros((b, 2, npairs, LANES, LANES), F32)
    else:
        s = s0.astype(F32).reshape(b, 2, npairs, 2, HEAD_DIM, HEAD_DIM)
        zero = jnp.zeros_like(s[:, :, :, 0])
        st = jnp.concatenate([jnp.concatenate([s[:, :, :, 0], zero], axis=-1),
                              jnp.concatenate([zero, s[:, :, :, 1]], axis=-1)], axis=-2)
    o2, s_fin = _rwkv_scan(ops, st)
    s_fin = jnp.stack([s_fin[..., :HEAD_DIM, :HEAD_DIM], s_fin[..., HEAD_DIM:, HEAD_DIM:]], axis=3)
    s_fin = s_fin.reshape(b, 2, nh, HEAD_DIM, HEAD_DIM)
    mixed = _rwkv_post(o2, bonus, z, dh, pe["gn_g"], pe["gn_b"])
    return mixed, s_fin


def _even_layer(x, sc, sh, gt, ln_g, ln_b, alpha, pe, kv_ctx, s0):
    dh = pe["dh"]
    z = _modmm(x, sc, sh, pe["w_in"])
    if kv_ctx is None:
        mixed_a = _ctx_attn(z, dh)
    else:
        mixed_a = _na_attn(z, dh, kv_ctx[0], kv_ctx[1], pe["bias"])
    mixed_b, s_fin = _rwkv_heads(z, dh, pe, s0)
    out = _outln([mixed_a, mixed_b], [pe["w_out"][:dh], pe["w_out"][dh:]], x, gt, ln_g, ln_b, alpha)
    return out, z, s_fin


def _odd_layer(x, sc, sh, gt, ln_g, ln_b, alpha, po, spec, mats):
    c = po["c"]
    b, l, _ = x.shape
    z = _modmm(x, sc, sh, po["w_in"], planes=HY_PLANES)
    x0, uu, uub = _hy_conv(z, po["conv_w"], po["conv_b"], c)
    mixed = _hy_fftconv(mats, spec, uub, uu, x0, z, po["skip"])
    mixed = mixed.transpose(0, 2, 1, 3).reshape(b, l, c)
    return _outln([mixed], [po["w_out"]], x, gt, ln_g, ln_b, alpha)


def kernel(x_prompt, x_sample, c, cache_a_k, cache_a_v, state_b, c_ctx, w_ada, b_ada, ln_g, ln_b,
           w_in_even, w_out_even, a_rpb, b_conv_w, b_w0, b_wup, b_a0, b_aup, b_kk, b_ka, b_rk,
           b_gn_g, b_gn_b, w_in_odd, w_out_odd, h_conv_w, h_conv_b, h_ffn_w1, h_ffn_b1, h_ffn_w2,
           h_ffn_b2, h_ffn_w3, h_ffn_b3, h_freq, h_ffn_wout, h_skip):
    depth, d, _ = w_ada.shape
    bx, lx, _ = x_prompt.shape
    by, ly, _ = x_sample.shape
    alpha = (2 * depth) ** 0.25
    dh = b_kk.shape[1]
    nh = dh // HEAD_DIM
    p_len = cache_a_k.shape[2]
    c_hy = h_skip.shape[1]

    rows = 1 + by
    rpad = -(-rows // SUBLANES) * SUBLANES
    cond = jnp.pad(jnp.concatenate([c_ctx[None, :], c], axis=0), ((0, rpad - rows), (0, 0)))
    mods = _ada_mod(cond, w_ada, b_ada)

    x = x_prompt
    y = x_sample
    mats_x = _hy_mats(lx) if depth > 1 else None
    mats_y = _hy_mats(ly) if depth > 1 else None
    new_k, new_v, new_s = [], [], []
    for l in range(depth):
        m = mods[l]
        sh_x, sc_x, gt_x = (jnp.broadcast_to(m[0:1, k * d:(k + 1) * d][None], (bx, 1, d)) for k in range(3))
        sh_y, sc_y, gt_y = (m[1:rows, k * d:(k + 1) * d][:, None, :] for k in range(3))
        if l % 2 == 0:
            e = l // 2
            pe = dict(dh=dh, w_in=w_in_even[e].astype(BF16), w_out=w_out_even[e].astype(BF16),
                      bias=_na_bias_table(a_rpb[e]), conv_w=b_conv_w[e], w0=b_w0[e], wup=b_wup[e], a0=b_a0[e],
                      aup=b_aup[e], k_k=b_kk[e], k_a=b_ka[e], r_k=b_rk[e], gn_g=b_gn_g[e], gn_b=b_gn_b[e])
            x, zx, s_fin = _even_layer(x, sc_x, sh_x, gt_x, ln_g[l], ln_b[l], alpha, pe, None, None)
            new_k.append(zx[..., dh:2 * dh].reshape(bx, lx, nh, HEAD_DIM))
            new_v.append(zx[..., 2 * dh:3 * dh].reshape(bx, lx, nh, HEAD_DIM))
            new_s.append(s_fin)
            kv = (cache_a_k[:, e].reshape(by, p_len, dh), cache_a_v[:, e].reshape(by, p_len, dh))
            y, _, _ = _even_layer(y, sc_y, sh_y, gt_y, ln_g[l], ln_b[l], alpha, pe, kv, state_b[:, e])
        else:
            o = l // 2
            po = dict(c=c_hy, w_in=w_in_odd[o].astype(BF16), w_out=w_out_odd[o].astype(BF16),
                      conv_w=h_conv_w[o], conv_b=h_conv_b[o], skip=h_skip[o])
            fargs = (h_ffn_w1[o], h_ffn_b1[o], h_ffn_w2[o], h_ffn_b2[o], h_ffn_w3[o], h_ffn_b3[o],
                     h_freq[o], h_ffn_wout[o])
            spec_x = _hy_spectrum(lx, c_hy, mats_x[0], *fargs)
            spec_y = _hy_spectrum(ly, c_hy, mats_y[0], *fargs)
            x = _odd_layer(x, sc_x, sh_x, gt_x, ln_g[l], ln_b[l], alpha, po, spec_x, mats_x)
            y = _odd_layer(y, sc_y, sh_y, gt_y, ln_g[l], ln_b[l], alpha, po, spec_y, mats_y)
    return (x, y, jnp.stack(new_k, axis=1), jnp.stack(new_v, axis=1), jnp.stack(new_s, axis=1))
```

```python
import functools
import math

import jax
import jax.numpy as jnp
from jax import lax
from jax.experimental import pallas as pl
from jax.experimental.pallas import tpu as pltpu

F32 = jnp.float32
BF16 = jnp.bfloat16

HEAD_DIM = 64
LORA = 64
GRID_W = 64
WIN_H = 8
WIN_W = 16
HY_TARGET = 1e-2
HY_FAST = 0.3
HY_SLOW = 1.5
LN_EPS = 1e-5
GN_EPS = 64e-5
NEG = -1e30

LANES = 128
SUBLANES = 8
VMEM_LIMIT = 56 * 1024 * 1024


def _tile(n, target, mult):
    if n <= target:
        return n
    t = (target // mult) * mult
    while t >= mult:
        if n % t == 0:
            return t
        t -= mult
    return n


def _params(*sem):
    return pltpu.CompilerParams(dimension_semantics=sem, vmem_limit_bytes=VMEM_LIMIT)


def _silu(x):
    return x * jax.nn.sigmoid(x)


def _dot(a, b):
    return jnp.dot(a, b, preferred_element_type=F32)


def _dot_nt(a, b):
    return lax.dot_general(a, b, (((1,), (1,)), ((), ())), preferred_element_type=F32)


def _run_lockstep(gens):
    results = [None] * len(gens)
    active = list(range(len(gens)))
    while active:
        for p in list(active):
            try:
                next(gens[p])
            except StopIteration as done:
                results[p] = done.value
                active.remove(p)
    return results


def _head_ones():
    r = lax.broadcasted_iota(jnp.int32, (LANES, LANES), 0) // HEAD_DIM
    c = lax.broadcasted_iota(jnp.int32, (LANES, LANES), 1) // HEAD_DIM
    return (r == c).astype(BF16)


def _head_sum(x, ones):
    hi = x.astype(BF16)
    r1 = x - hi.astype(F32)
    mid = r1.astype(BF16)
    lo = (r1 - mid.astype(F32)).astype(BF16)
    return _dot(hi, ones) + _dot(mid, ones) + _dot(lo, ones)


def _ada_kernel(c_ref, w_ref, b_ref, o_ref):
    c = c_ref[...]
    o_ref[...] = _dot(_silu(c).astype(BF16), w_ref[...].astype(BF16)) + b_ref[...]


def _ada_mod(cond, w_ada, b_ada):
    depth, d, n = w_ada.shape
    r = cond.shape[0]
    tn = _tile(n, 768, LANES)
    return pl.pallas_call(
        _ada_kernel,
        grid=(depth, n // tn),
        in_specs=[pl.BlockSpec((r, d), lambda l, j: (0, 0)),
                  pl.BlockSpec((None, d, tn), lambda l, j: (l, 0, j)),
                  pl.BlockSpec((None, 1, tn), lambda l, j: (l, 0, j))],
        out_specs=pl.BlockSpec((None, r, tn), lambda l, j: (l, 0, j)),
        out_shape=jax.ShapeDtypeStruct((depth, r, n), F32),
        compiler_params=_params("parallel", "parallel"),
        name="ada_mod",
    )(cond, w_ada, b_ada.reshape(depth, 1, n))


def _modmm_kernel(*refs, nx, plane_out):
    x_refs = refs[:nx]
    sc_ref, sh_ref, w_ref, o_ref, h_ref = refs[nx:]
    rows = x_refs[0].shape[0]

    @pl.when(pl.program_id(2) == 0)
    def _():
        for r in range(nx):
            h_ref[r * rows:(r + 1) * rows, :] = (x_refs[r][...] * (1.0 + sc_ref[...]) + sh_ref[...]).astype(BF16)

    out = _dot(h_ref[...], w_ref[...])
    if plane_out:
        for r in range(nx):
            o_ref[r] = out[r * rows:(r + 1) * rows]
    else:
        o_ref[...] = out


def _modmm(x, sc, sh, w, planes=None):
    b, l, d = x.shape
    n = w.shape[1]
    tn = _tile(n, 1024, LANES)
    if planes is None:
        rows = _tile(l, 1024, SUBLANES)
        nx = 1
        grid = (b, l // rows, n // tn)
        xs = [x]
        x_specs = [pl.BlockSpec((None, rows, d), lambda bb, i, j: (bb, i, 0))]
        out_spec = pl.BlockSpec((None, rows, tn), lambda bb, i, j: (bb, i, j))
        out_shape = (b, l, n)
    else:
        rows = l // planes
        nx = max(1, min(planes, 1024 // rows))
        assert l % planes == 0 and planes % nx == 0
        grid = (b, planes // nx, n // tn)
        xs = [x.reshape(b, rows, planes * d)] * nx
        x_specs = [pl.BlockSpec((None, rows, d), lambda bb, i, j, r=r: (bb, 0, i * nx + r)) for r in range(nx)]
        out_spec = pl.BlockSpec((None, nx, rows, tn), lambda bb, i, j: (bb, i, 0, j))
        out_shape = (b, planes, rows, n)
    return pl.pallas_call(
        functools.partial(_modmm_kernel, nx=nx, plane_out=planes is not None),
        grid=grid,
        in_specs=x_specs + [pl.BlockSpec((None, 1, d), lambda bb, i, j: (bb, 0, 0)),
                            pl.BlockSpec((None, 1, d), lambda bb, i, j: (bb, 0, 0)),
                            pl.BlockSpec((d, tn), lambda bb, i, j: (0, j))],
        out_specs=out_spec,
        out_shape=jax.ShapeDtypeStruct(out_shape, F32),
        scratch_shapes=[pltpu.VMEM((nx * rows, d), BF16)],
        compiler_params=_params("parallel", "parallel", "arbitrary"),
        name="mod_in_proj",
    )(*xs, sc, sh, w)


def _outln_kernel(*refs, nparts, alpha):
    m_refs = refs[:nparts]
    w_refs = refs[nparts:2 * nparts]
    x_ref, gt_ref, g_ref, b_ref, o_ref = refs[2 * nparts:]
    acc = _dot(m_refs[0][...], w_refs[0][...])
    for p in range(1, nparts):
        acc = acc + _dot(m_refs[p][...], w_refs[p][...])
    y = alpha * x_ref[...] + gt_ref[...] * acc
    mu = jnp.mean(y, axis=-1, keepdims=True)
    yc = y - mu
    var = jnp.mean(yc * yc, axis=-1, keepdims=True)
    o_ref[...] = yc * lax.rsqrt(var + LN_EPS) * g_ref[...] + b_ref[...]


def _outln(parts, ws, x, gt, ln_g, ln_b, alpha, planes=None):
    b, l, d = x.shape
    nparts = len(parts)
    if planes is None:
        tl = _tile(l, 512, SUBLANES)
        grid = (b, 1, l // tl)
        part_specs = [pl.BlockSpec((None, tl, p.shape[2]), lambda bb, t2, i: (bb, i, 0)) for p in parts]
        row_spec = pl.BlockSpec((None, tl, d), lambda bb, t2, i: (bb, i, 0))
        xv = x
    else:
        rows = l // planes
        tl = _tile(rows, 512, SUBLANES)
        grid = (b, planes, rows // tl)
        part_specs = [pl.BlockSpec((None, None, tl, p.shape[3]), lambda bb, t2, i: (bb, t2, i, 0)) for p in parts]
        row_spec = pl.BlockSpec((None, tl, d), lambda bb, t2, i: (bb, i, t2))
        xv = x.reshape(b, rows, planes * d)
    in_specs = part_specs + [pl.BlockSpec(w.shape, lambda bb, t2, i: (0, 0)) for w in ws]
    in_specs += [row_spec,
                 pl.BlockSpec((None, 1, d), lambda bb, t2, i: (bb, 0, 0)),
                 pl.BlockSpec((1, d), lambda bb, t2, i: (0, 0)),
                 pl.BlockSpec((1, d), lambda bb, t2, i: (0, 0))]
    out = pl.pallas_call(
        functools.partial(_outln_kernel, nparts=nparts, alpha=alpha),
        grid=grid,
        in_specs=in_specs,
        out_specs=row_spec,
        out_shape=jax.ShapeDtypeStruct(xv.shape, F32),
        compiler_params=_params("parallel", "parallel", "parallel"),
        name="out_proj_ln",
    )(*parts, *ws, xv, gt, ln_g.reshape(1, d), ln_b.reshape(1, d))
    return out.reshape(b, l, d)


def _conv_specs(tl, tc, l, colblk):
    g = tl // SUBLANES
    last = l // SUBLANES - 1
    return [
        pl.BlockSpec((None, tl, tc), lambda bb, i, j: (bb, i, colblk + j)),
        pl.BlockSpec((None, SUBLANES, tc), lambda bb, i, j: (bb, jnp.maximum(i * g - 1, 0), colblk + j)),
        pl.BlockSpec((None, SUBLANES, tc), lambda bb, i, j: (bb, jnp.minimum((i + 1) * g, last), colblk + j)),
    ]


def _conv3(x_ref, p_ref, n_ref, w, first, last, lanes=slice(None)):
    x = x_ref[:, lanes]
    tl = x.shape[0]
    row = lax.broadcasted_iota(jnp.int32, x.shape, 0)
    prow = jnp.where(first, 0.0, p_ref[SUBLANES - 1:SUBLANES, lanes])
    nrow = jnp.where(last, 0.0, n_ref[0:1, lanes])
    xm = jnp.where(row == 0, prow, pltpu.roll(x, 1, 0))
    xq = jnp.where(row == tl - 1, nrow, pltpu.roll(x, tl - 1, 0))
    return xm * w[0:1, :] + x * w[1:2, :] + xq * w[2:3, :]


HY_PLANES = 16
HY_PLANE_ROWS = 256


def _hy_planes(l):
    p = HY_PLANES
    while p > 1 and l // p < HY_PLANE_ROWS:
        p //= 2
    return p


def _plane_conv3(z_ref, w, bias):
    p, rows, _ = z_ref.shape
    row = lax.broadcasted_iota(jnp.int32, z_ref.shape[1:], 0)
    before = jnp.where(row == 0, 0.0, pltpu.roll(z_ref[p - 1], 1, 0))
    after = jnp.where(row == rows - 1, 0.0, pltpu.roll(z_ref[0], rows - 1, 0))
    out = []
    for t2 in range(p):
        prev = z_ref[t2 - 1] if t2 > 0 else before
        nxt = z_ref[t2 + 1] if t2 < p - 1 else after
        out.append(prev * w[0:1, :] + z_ref[t2] * w[1:2, :] + nxt * w[2:3, :] + bias)
    return out


def _hy_conv_kernel(z0_ref, z1_ref, z2_ref, w0_ref, w1_ref, w2_ref, b0_ref, b1_ref, b2_ref,
                    x0o_ref, uu_ref, uub_ref):
    x0 = _plane_conv3(z0_ref, w0_ref[...], b0_ref[...])
    x1 = _plane_conv3(z1_ref, w1_ref[...], b1_ref[...])
    v = _plane_conv3(z2_ref, w2_ref[...], b2_ref[...])
    for t2 in range(len(x0)):
        uu = x1[t2] * v[t2]
        x0o_ref[t2] = x0[t2]
        uu_ref[t2] = uu
        uub_ref[t2] = uu.astype(BF16)


def _hy_conv(z, conv_w, conv_b, c):
    b, p, rows, _ = z.shape
    tc = _tile(c, 256, LANES)
    nc = c // tc
    in_specs = [pl.BlockSpec((None, p, rows, tc), lambda bb, j, k=k: (bb, 0, 0, k * nc + j)) for k in range(3)]
    in_specs += [pl.BlockSpec((3, tc), lambda bb, j, k=k: (0, k * nc + j)) for k in range(3)]
    in_specs += [pl.BlockSpec((1, tc), lambda bb, j, k=k: (0, k * nc + j)) for k in range(3)]
    ospec = pl.BlockSpec((None, p, rows, tc), lambda bb, j: (bb, 0, 0, j))
    cb = conv_b.reshape(1, 3 * c)
    return pl.pallas_call(
        _hy_conv_kernel,
        grid=(b, nc),
        in_specs=in_specs,
        out_specs=[ospec, ospec, ospec],
        out_shape=[jax.ShapeDtypeStruct((b, p, rows, c), F32), jax.ShapeDtypeStruct((b, p, rows, c), F32),
                   jax.ShapeDtypeStruct((b, p, rows, c), BF16)],
        compiler_params=_params("parallel", "parallel"),
        name="hyena_gate_conv",
    )(z, z, z, conv_w, conv_w, conv_w, cb, cb, cb)


def _hy_filter_kernel(feats_ref, t_ref, fw1_ref, fb1_ref, fw2_ref, fb2_ref, fw3_ref, fb3_ref, freq_ref,
                      wf_ref, wb_ref, dl_ref, o_ref):
    freq = freq_ref[...]
    hdn = jnp.sin(freq[0:1, :] * (_dot(feats_ref[...].astype(BF16), fw1_ref[...].astype(BF16)) + fb1_ref[...]))
    hdn = jnp.sin(freq[1:2, :] * (_dot(hdn.astype(BF16), fw2_ref[...].astype(BF16)) + fb2_ref[...]))
    hdn = jnp.sin(freq[2:3, :] * (_dot(hdn.astype(BF16), fw3_ref[...].astype(BF16)) + fb3_ref[...]))
    hb16 = hdn.astype(BF16)
    window = jnp.exp(-t_ref[...] * dl_ref[...])
    h_f = _dot(hb16, wf_ref[...].astype(BF16)) * window
    h_b = _dot(hb16, wb_ref[...].astype(BF16)) * window
    h_b = h_b * (t_ref[...] > 0.0).astype(F32)
    norm = jnp.sum(jnp.abs(h_f), axis=0, keepdims=True) + jnp.sum(jnp.abs(h_b), axis=0, keepdims=True)
    inv = 1.0 / norm
    o_ref[0] = (h_f * inv).astype(BF16)
    o_ref[1] = (h_b * inv).astype(BF16)


def _hy_filter(l, c, fw1, fb1, fw2, fb2, fw3, fb3, freq, fwout):
    emb, order = fw1.shape
    bands_n = (emb - 1) // 2
    t = jnp.linspace(0.0, 1.0, l, dtype=F32)[:, None]
    ang = 2.0 * math.pi * jnp.arange(l, dtype=F32)[:, None] / l
    bands = jnp.linspace(1e-4, bands_n - 1, bands_n, dtype=F32)
    feats = jnp.concatenate([t, jnp.cos(ang * bands), -jnp.sin(ang * bands)], axis=-1)
    feats = jnp.pad(feats, ((0, 0), (0, LANES - emb)))
    p = _hy_planes(l)
    plane_major = lambda a: a.reshape(l // p, p, a.shape[1]).transpose(1, 0, 2).reshape(l, a.shape[1])
    feats, t = plane_major(feats), plane_major(t)
    fw1p = jnp.pad(fw1, ((0, LANES - emb), (0, 0)))
    deltas = jnp.abs(jnp.linspace(math.log(HY_TARGET) / HY_SLOW, math.log(HY_TARGET) / HY_FAST, c, dtype=F32))
    tc = _tile(c, 256, LANES)
    nc = c // tc
    full = lambda shape: pl.BlockSpec(shape, lambda j: tuple(0 for _ in shape))
    return pl.pallas_call(
        _hy_filter_kernel,
        grid=(nc,),
        in_specs=[full((l, LANES)), full((l, 1)), full((LANES, order)), full((1, order)),
                  full((order, order)), full((1, order)), full((order, order)), full((1, order)),
                  full((3, order)),
                  pl.BlockSpec((order, tc), lambda j: (0, j)),
                  pl.BlockSpec((order, tc), lambda j: (0, nc + j)),
                  pl.BlockSpec((1, tc), lambda j: (0, j))],
        out_specs=pl.BlockSpec((2, l, tc), lambda j: (0, 0, j)),
        out_shape=jax.ShapeDtypeStruct((2, l, c), BF16),
        compiler_params=_params("parallel"),
        name="hyena_filter",
    )(feats, t, fw1p, fb1.reshape(1, order), fw2, fb2.reshape(1, order), fw3, fb3.reshape(1, order),
      freq, fwout, fwout, deltas.reshape(1, c))


def _hy_mats(l):
    p = _hy_planes(l)
    rows = l // p
    n = 2 * l
    kq = jnp.arange(rows, dtype=jnp.int32)
    t = p * jnp.arange(rows, dtype=jnp.int32)[None, None, :] + jnp.arange(p, dtype=jnp.int32)[:, None, None]
    phase = ((2 * kq[None, :, None] + 1) * t) % (2 * n)
    ang = phase.astype(F32) * (math.pi / n)
    cos, sin = jnp.cos(ang), jnp.sin(ang)
    fwd = jnp.concatenate([cos, -sin], axis=1)
    inv = (2.0 / n) * jnp.concatenate([cos.transpose(0, 2, 1), -sin.transpose(0, 2, 1)], axis=2)
    return fwd.astype(BF16), inv.astype(BF16)


def _plane_butterflies(s_ref, sign, dif):
    p = s_ref.shape[1]
    sizes = [p >> i for i in range(int(math.log2(p)))]
    if not dif:
        sizes = sizes[::-1]
    for size in sizes:
        half = size // 2
        for start in range(0, p, size):
            for k in range(half):
                ang = sign * 2.0 * math.pi * k / size
                wr, wi = math.cos(ang), math.sin(ang)
                i0, i1 = start + k, start + k + half
                ur, ui, vr, vi = s_ref[0, i0], s_ref[1, i0], s_ref[0, i1], s_ref[1, i1]

                def twiddle(xr, xi):
                    if k == 0:
                        return xr, xi
                    if 4 * k == size:
                        return (-xi, xr) if sign > 0 else (xi, -xr)
                    return xr * wr - xi * wi, xr * wi + xi * wr

                if dif:
                    s_ref[0, i0] = ur + vr
                    s_ref[1, i0] = ui + vi
                    tr, ti = twiddle(ur - vr, ui - vi)
                    s_ref[0, i1] = tr
                    s_ref[1, i1] = ti
                else:
                    tr, ti = twiddle(vr, vi)
                    s_ref[0, i0] = ur + tr
                    s_ref[1, i0] = ui + ti
                    s_ref[0, i1] = ur - tr
                    s_ref[1, i1] = ui - ti


def _plane_spectrum(x_ref, f_ref, s_ref):
    p, rows, _ = x_ref.shape
    for t2 in range(p):
        y = _dot(f_ref[t2], x_ref[t2])
        s_ref[0, t2] = y[:rows]
        s_ref[1, t2] = y[rows:]
    _plane_butterflies(s_ref, -1, dif=True)


def _hy_spec_kernel(h_ref, f_ref, o_ref, s_ref):
    p = s_ref.shape[1]
    _plane_spectrum(h_ref.at[0], f_ref, s_ref)
    for i in range(p):
        o_ref[0, i] = s_ref[0, i]
        o_ref[1, i] = s_ref[1, i]
    _plane_spectrum(h_ref.at[1], f_ref, s_ref)
    for i in range(p):
        o_ref[0, i] = o_ref[0, i] + s_ref[0, i]
        o_ref[1, i] = o_ref[1, i] - s_ref[1, i]


def _hy_spectrum(l, c, fwd, fw1, fb1, fw2, fb2, fw3, fb3, freq, fwout):
    p = _hy_planes(l)
    rows = l // p
    hfb = _hy_filter(l, c, fw1, fb1, fw2, fb2, fw3, fb3, freq, fwout).reshape(2, p, rows, c)
    tn = LANES
    return pl.pallas_call(
        _hy_spec_kernel,
        grid=(c // tn,),
        in_specs=[pl.BlockSpec((2, p, rows, tn), lambda j: (0, 0, 0, j)),
                  pl.BlockSpec((p, 2 * rows, rows), lambda j: (0, 0, 0))],
        out_specs=pl.BlockSpec((2, p, rows, tn), lambda j: (0, 0, 0, j)),
        out_shape=jax.ShapeDtypeStruct((2, p, rows, c), F32),
        scratch_shapes=[pltpu.VMEM((2, p, rows, tn), F32)],
        compiler_params=_params("parallel"),
        name="hyena_filter_spectrum",
    )(hfb, fwd)


def _hy_fftconv_kernel(x_ref, f_ref, g_ref, h_ref, uu_ref, x0_ref, gz_ref, skip_ref, o_ref, s_ref):
    p, rows, _ = x_ref.shape
    _plane_spectrum(x_ref, f_ref, s_ref)
    for i in range(p):
        re, im, hre, him = s_ref[0, i], s_ref[1, i], h_ref[0, i], h_ref[1, i]
        s_ref[0, i] = re * hre - im * him
        s_ref[1, i] = re * him + im * hre
    _plane_butterflies(s_ref, 1, dif=False)
    skip = skip_ref[...]
    for t2 in range(p):
        spec = jnp.concatenate([s_ref[0, t2], s_ref[1, t2]], axis=0).astype(BF16)
        y = _dot(g_ref[t2], spec) + uu_ref[t2] * skip
        o_ref[t2] = (_silu(gz_ref[t2]) * (x0_ref[t2] * y)).astype(BF16)


def _hy_fftconv(mats, spec, uub, uu, x0, z, skip):
    b, p, rows, c = uu.shape
    tn = LANES
    nc = c // tn
    tile = pl.BlockSpec((None, p, rows, tn), lambda bb, j: (bb, 0, 0, j))
    return pl.pallas_call(
        _hy_fftconv_kernel,
        grid=(b, nc),
        in_specs=[tile,
                  pl.BlockSpec((p, 2 * rows, rows), lambda bb, j: (0, 0, 0)),
                  pl.BlockSpec((p, rows, 2 * rows), lambda bb, j: (0, 0, 0)),
                  pl.BlockSpec((2, p, rows, tn), lambda bb, j: (0, 0, 0, j)),
                  tile, tile,
                  pl.BlockSpec((None, p, rows, tn), lambda bb, j: (bb, 0, 0, 3 * nc + j)),
                  pl.BlockSpec((1, tn), lambda bb, j: (0, j))],
        out_specs=tile,
        out_shape=jax.ShapeDtypeStruct((b, p, rows, c), BF16),
        scratch_shapes=[pltpu.VMEM((2, p, rows, tn), F32)],
        compiler_params=_params("parallel", "parallel"),
        name="hyena_fftconv",
    )(uub, mats[0], mats[1], spec, uu, x0, z, skip.reshape(1, c))


def _lane_mask(shape):
    return lax.broadcasted_iota(jnp.int32, shape, 1) < HEAD_DIM


def _ctx_attn_kernel(q_ref, k_ref, v_ref, ga_ref, o_ref):
    q = q_ref[...]
    k = k_ref[...].astype(BF16)
    v = v_ref[...].astype(BF16)
    m0 = _lane_mask(q.shape)
    scale = HEAD_DIM ** -0.5
    outs = []
    for h in range(2):
        qh = jnp.where(m0 if h == 0 else jnp.logical_not(m0), q, 0.0).astype(BF16)
        s = _dot_nt(qh, k) * scale
        p = jnp.exp(s - jnp.max(s, axis=-1, keepdims=True))
        den = jnp.sum(p, axis=-1, keepdims=True)
        outs.append(_dot(p.astype(BF16), v) / den)
    o = jnp.where(m0, outs[0], outs[1])
    o_ref[...] = (o * _silu(ga_ref[...])).astype(BF16)


def _ctx_attn(z, dh):
    b, l, _ = z.shape
    nb = dh // LANES
    spec = lambda g: pl.BlockSpec((None, l, LANES), lambda bb, hp, g=g: (bb, 0, g * nb + hp))
    return pl.pallas_call(
        _ctx_attn_kernel,
        grid=(b, nb),
        in_specs=[spec(0), spec(1), spec(2), spec(3)],
        out_specs=pl.BlockSpec((None, l, LANES), lambda bb, hp: (bb, 0, hp)),
        out_shape=jax.ShapeDtypeStruct((b, l, dh), BF16),
        compiler_params=_params("parallel", "parallel"),
        name="ctx_attention",
    )(z, z, z, z)


def _na_bias_table(rpb):
    c = jnp.arange(GRID_W)[:, None]
    kc = jnp.arange(GRID_W)[None, :]
    qstart = jnp.clip(c - WIN_W // 2, 0, GRID_W - WIN_W)
    mask = (kc >= qstart) & (kc < qstart + WIN_W)
    cidx = jnp.clip(kc - c, 1 - WIN_W, WIN_W - 1) + WIN_W - 1
    dr = jnp.arange(WIN_H)[:, None] + jnp.arange(WIN_H)[None, :]
    t = rpb[:, dr][..., cidx]
    t = jnp.where(mask, t, NEG)
    nh = rpb.shape[0]
    t = t.transpose(1, 0, 3, 2, 4).reshape(WIN_H, nh // 2, 2 * GRID_W, WIN_H * GRID_W)
    return t.transpose(1, 0, 2, 3)


NA_ROWS_PER_STEP = 4


def _na_kernel(q_ref, k_ref, v_ref, kc_ref, vc_ref, bias_ref, ga_ref, o_ref, kb_ref, vb_ref, *, rows):
    kb_ref[...] = k_ref[...].astype(BF16)
    vb_ref[...] = v_ref[...].astype(BF16)
    kc = kc_ref[...].astype(BF16)
    vc = vc_ref[...].astype(BF16)
    m0 = _lane_mask((GRID_W, LANES))
    scale = HEAD_DIM ** -0.5
    win = WIN_H * GRID_W

    def one_row(r):
        rs = jnp.clip(r - WIN_H // 2, 0, rows - WIN_H)
        d0 = rs - r + WIN_H - 1
        q0 = pl.multiple_of(r * GRID_W, GRID_W)
        k0 = pl.multiple_of(rs * GRID_W, GRID_W)
        q = q_ref[pl.ds(q0, GRID_W), :]
        q2 = jnp.concatenate([jnp.where(m0, q, 0.0), jnp.where(m0, 0.0, q)], axis=0).astype(BF16)
        kw = kb_ref[pl.ds(k0, win), :]
        vw = vb_ref[pl.ds(k0, win), :]
        s_nb = _dot_nt(q2, kw)
        s_cx = _dot_nt(q2, kc)
        yield
        s_nb = s_nb * scale + bias_ref[d0]
        s_cx = s_cx * scale
        m = jnp.maximum(jnp.max(s_nb, axis=-1, keepdims=True), jnp.max(s_cx, axis=-1, keepdims=True))
        p_nb = jnp.exp(s_nb - m)
        p_cx = jnp.exp(s_cx - m)
        den = jnp.sum(p_nb, axis=-1, keepdims=True) + jnp.sum(p_cx, axis=-1, keepdims=True)
        o2 = _dot(p_nb.astype(BF16), vw) + _dot(p_cx.astype(BF16), vc)
        yield
        o2 = o2 / den
        o = jnp.where(m0, o2[:GRID_W], o2[GRID_W:])
        ga = ga_ref[pl.ds(q0, GRID_W), :]
        o_ref[pl.ds(q0, GRID_W), :] = (o * _silu(ga)).astype(BF16)

    def body(i, carry):
        _run_lockstep([one_row(i * NA_ROWS_PER_STEP + u) for u in range(NA_ROWS_PER_STEP)])
        return carry

    lax.fori_loop(0, rows // NA_ROWS_PER_STEP, body, 0)


def _na_attn(z, dh, k_ctx, v_ctx, bias):
    b, t, _ = z.shape
    rows = t // GRID_W
    assert t % GRID_W == 0 and rows >= WIN_H and rows % NA_ROWS_PER_STEP == 0
    p = k_ctx.shape[1]
    nb = dh // LANES
    spec = lambda g: pl.BlockSpec((None, t, LANES), lambda bb, hp, g=g: (bb, 0, g * nb + hp))
    cspec = pl.BlockSpec((None, p, LANES), lambda bb, hp: (bb, 0, hp))
    return pl.pallas_call(
        functools.partial(_na_kernel, rows=rows),
        grid=(b, nb),
        in_specs=[spec(0), spec(1), spec(2), cspec, cspec,
                  pl.BlockSpec((None, WIN_H, 2 * GRID_W, WIN_H * GRID_W), lambda bb, hp: (hp, 0, 0, 0)),
                  spec(3)],
        out_specs=pl.BlockSpec((None, t, LANES), lambda bb, hp: (bb, 0, hp)),
        out_shape=jax.ShapeDtypeStruct((b, t, dh), BF16),
        scratch_shapes=[pltpu.VMEM((t, LANES), BF16), pltpu.VMEM((t, LANES), BF16)],
        compiler_params=_params("parallel", "parallel"),
        name="na_attention",
    )(z, z, z, k_ctx, v_ctx, bias, z)


N_OPS = 9


def _softplus(x):
    return jnp.maximum(x, 0.0) + jnp.log(1.0 + jnp.exp(-jnp.abs(x)))


def _rwkv_prep_kernel(r_ref, rp_ref, rn_ref, k_ref, kp_ref, kn_ref, v_ref, vp_ref, vn_ref,
                      cwr_ref, cwk_ref, cwv_ref, wdn_ref, adn_ref, wup_ref, aup_ref, w0_ref, a0_ref,
                      kk_ref, ka_ref, rk_ref, o_ref, bonus_ref):
    first = pl.program_id(1) == 0
    last = pl.program_id(1) == pl.num_programs(1) - 1
    ones = _head_ones()
    tw = jnp.tanh(wdn_ref[...]).astype(BF16)
    ad = adn_ref[...].astype(BF16)
    for s in range(bonus_ref.shape[1] // LANES):
        sl = slice(s * LANES, (s + 1) * LANES)
        r = _conv3(r_ref, rp_ref, rn_ref, cwr_ref[:, sl], first, last, sl)
        k = _conv3(k_ref, kp_ref, kn_ref, cwk_ref[:, sl], first, last, sl)
        v = _conv3(v_ref, vp_ref, vn_ref, cwv_ref[:, sl], first, last, sl)
        k_a = ka_ref[:, sl]
        kk = k * kk_ref[:, sl]
        nrm = jnp.sqrt(_head_sum(kk * kk, ones))
        kk = kk / jnp.maximum(nrm, 1e-12)
        o_ref[0, :, sl] = r
        o_ref[1, :, sl] = v
        o_ref[2, :, sl] = kk
        kd_sum = None
        for d in range(2):
            w_raw = -_softplus(-(w0_ref[d:d + 1, sl] + _dot(tw, wup_ref[d, :, sl].astype(BF16)))) - 0.5
            a = jax.nn.sigmoid(a0_ref[d:d + 1, sl] + _dot(ad, aup_ref[d, :, sl].astype(BF16)))
            kd = k * (1.0 + (a - 1.0) * k_a)
            o_ref[3 + d, :, sl] = -jnp.exp(w_raw)
            o_ref[5 + d, :, sl] = kd
            o_ref[7 + d, :, sl] = kk * a
            kd_sum = kd if kd_sum is None else kd_sum + kd
        bonus_ref[:, sl] = _head_sum(r * kd_sum * rk_ref[:, sl], ones) * v


def _rwkv_prep(z, dh, conv_w, w0, wup, a0, aup, k_k, k_a, r_k):
    b, l, _ = z.shape
    tl = _tile(l, 512, SUBLANES)
    tw = _tile(dh, 512, LANES)
    nb = dh // tw
    base = 4 * nb
    lora_blk = 8 * dh // LANES
    in_specs = (_conv_specs(tl, tw, l, base) + _conv_specs(tl, tw, l, base + nb)
                + _conv_specs(tl, tw, l, base + 2 * nb))
    in_specs += [pl.BlockSpec((3, tw), lambda bb, i, j, g=g: (0, g * nb + j)) for g in range(3)]
    in_specs += [pl.BlockSpec((None, tl, LANES), lambda bb, i, j: (bb, i, lora_blk)),
                 pl.BlockSpec((None, tl, LANES), lambda bb, i, j: (bb, i, lora_blk + 1)),
                 pl.BlockSpec((2, LANES, tw), lambda bb, i, j: (0, 0, j)),
                 pl.BlockSpec((2, LANES, tw), lambda bb, i, j: (0, 0, j)),
                 pl.BlockSpec((2, tw), lambda bb, i, j: (0, j)),
                 pl.BlockSpec((2, tw), lambda bb, i, j: (0, j)),
                 pl.BlockSpec((1, tw), lambda bb, i, j: (0, j)),
                 pl.BlockSpec((1, tw), lambda bb, i, j: (0, j)),
                 pl.BlockSpec((1, tw), lambda bb, i, j: (0, j))]
    zeros = jnp.zeros((LORA, dh), F32)
    wup_p = jnp.stack([jnp.concatenate([wup[0], zeros]), jnp.concatenate([zeros, wup[1]])])
    aup_p = jnp.stack([jnp.concatenate([aup[0], zeros]), jnp.concatenate([zeros, aup[1]])])
    return pl.pallas_call(
        _rwkv_prep_kernel,
        grid=(b, l // tl, nb),
        in_specs=in_specs,
        out_specs=[pl.BlockSpec((N_OPS, None, tl, tw), lambda bb, i, j: (0, bb, i, j)),
                   pl.BlockSpec((None, tl, tw), lambda bb, i, j: (bb, i, j))],
        out_shape=[jax.ShapeDtypeStruct((N_OPS, b, l, dh), F32), jax.ShapeDtypeStruct((b, l, dh), F32)],
        compiler_params=_params("parallel", "parallel", "parallel"),
        name="rwkv_prep",
    )(z, z, z, z, z, z, z, z, z, conv_w, conv_w, conv_w, z, z, wup_p, aup_p, w0, a0,
      k_k.reshape(1, dh), k_a.reshape(1, dh), r_k.reshape(1, dh))


SCAN_CHUNK = 64


def _split3(x):
    hi = x.astype(BF16)
    r1 = x - hi.astype(F32)
    mid = r1.astype(BF16)
    return hi, mid, (r1 - mid.astype(F32)).astype(BF16)


def _stack_heads(x, m0):
    return jnp.concatenate([jnp.where(m0, x, 0.0), jnp.where(m0, 0.0, x)], axis=0).astype(BF16)


def _scan_pair(lw, r, v, kk, kd, kb, s2, tri, strict2, incl2, m0, cm0, bd):
    c = lw.shape[0]
    hi, mid, lo = _split3(lw)
    a = _dot(tri, hi) + _dot(tri, mid) + _dot(tri, lo)
    yield
    atot = jnp.sum(lw, axis=0, keepdims=True)
    g_end = jnp.exp(atot - a)
    g_inv = jnp.exp(-a)
    kt = kk * jnp.exp(a - lw)
    rt = r * jnp.exp(a)
    lhs = jnp.concatenate([kt, rt], axis=0).astype(BF16)
    kbar = kd * g_inv
    bbar = kb * g_inv
    rhs = jnp.concatenate([_stack_heads(kbar, m0), _stack_heads(bbar, m0)], axis=0)
    g = _dot_nt(lhs, rhs)
    sh = _dot_nt(lhs, s2.astype(BF16))
    yield
    a_k = jnp.where(strict2, g[:c, :2 * c], 0.0)
    a_b = jnp.where(strict2, g[:c, 2 * c:], 0.0)
    a_rk = jnp.where(incl2, g[c:, :2 * c], 0.0)
    a_rb = jnp.where(incl2, g[c:, 2 * c:], 0.0)
    vs = _stack_heads(v, m0)
    x = sh[:c] + _dot(a_k.astype(BF16), vs)
    o_part = sh[c:] + _dot(a_rk.astype(BF16), vs)
    yield
    n = -a_b
    levels = int(math.log2(c))
    for lvl in range(levels):
        nb16 = n.astype(BF16)
        x = x + _dot(nb16, _stack_heads(x, m0))
        if lvl + 1 < levels:
            nbd = jnp.concatenate([jnp.where(cm0, n, 0.0), jnp.where(cm0, 0.0, n)], axis=0).astype(BF16)
            n = _dot(nb16, nbd)
        yield
    u = x
    o = o_part - _dot(a_rb.astype(BF16), _stack_heads(u, m0))
    w_val = jnp.concatenate([v, u], axis=0).astype(BF16)
    w_key = jnp.concatenate([kd * g_end, -(kb * g_end)], axis=0).astype(BF16)
    upd = lax.dot_general(w_val, w_key, (((0,), (0,)), ((), ())), preferred_element_type=F32)
    yield
    s_new = s2 * jnp.exp(atot) + jnp.where(bd, upd, 0.0)
    return o, s_new


def _scan_kernel(lw_ref, r_ref, v_ref, kk_ref, kd_ref, kb_ref, s0_ref, o_ref, sout_ref, s_ref, *, nchunk, npairs):
    d = pl.program_id(1)
    i = pl.program_id(2)

    @pl.when(i == 0)
    def _():
        s_ref[...] = s0_ref[...]

    c = SCAN_CHUNK
    sign = 1 - 2 * d
    tri = ((lax.broadcasted_iota(jnp.int32, (c, c), 1) - lax.broadcasted_iota(jnp.int32, (c, c), 0)) * sign
           <= 0).astype(BF16)
    diff2 = ((lax.broadcasted_iota(jnp.int32, (c, 2 * c), 1) & (c - 1))
             - lax.broadcasted_iota(jnp.int32, (c, 2 * c), 0)) * sign
    strict2 = diff2 < 0
    incl2 = diff2 <= 0
    m0 = _lane_mask((c, LANES))
    cm0 = lax.broadcasted_iota(jnp.int32, (c, 2 * c), 1) < c
    bd = (lax.broadcasted_iota(jnp.int32, (LANES, LANES), 0) // HEAD_DIM
          == lax.broadcasted_iota(jnp.int32, (LANES, LANES), 1) // HEAD_DIM)
    lanes = [slice(p * LANES, (p + 1) * LANES) for p in range(npairs)]
    results = _run_lockstep([
        _scan_pair(lw_ref[:, sl], r_ref[:, sl], v_ref[:, sl], kk_ref[:, sl], kd_ref[:, sl], kb_ref[:, sl],
                   s_ref[p], tri, strict2, incl2, m0, cm0, bd)
        for p, sl in enumerate(lanes)])
    for p, (o, s_new) in enumerate(results):
        o_ref[:, lanes[p]] = o
        s_ref[p] = s_new

    @pl.when(i == nchunk - 1)
    def _():
        sout_ref[...] = s_ref[...]


def _rwkv_scan(ops, s0):
    _, b, l, dh = ops.shape
    c = SCAN_CHUNK
    assert l % c == 0
    nchunk = l // c
    npairs = dh // LANES
    tb = lambda d, i: i + d * (nchunk - 1 - 2 * i)
    shared = lambda idx: pl.BlockSpec((None, None, c, dh), lambda bb, d, i: (idx, bb, tb(d, i), 0))
    perdir = lambda idx: pl.BlockSpec((None, None, c, dh), lambda bb, d, i: (idx + d, bb, tb(d, i), 0))
    sspec = pl.BlockSpec((None, None, npairs, LANES, LANES), lambda bb, d, i: (bb, d, 0, 0, 0))
    return pl.pallas_call(
        functools.partial(_scan_kernel, nchunk=nchunk, npairs=npairs),
        grid=(b, 2, nchunk),
        in_specs=[perdir(3), shared(0), shared(1), shared(2), perdir(5), perdir(7), sspec],
        out_specs=[pl.BlockSpec((None, None, c, dh), lambda bb, d, i: (d, bb, tb(d, i), 0)), sspec],
        out_shape=[jax.ShapeDtypeStruct((2, b, l, dh), F32),
                   jax.ShapeDtypeStruct((b, 2, npairs, LANES, LANES), F32)],
        scratch_shapes=[pltpu.VMEM((npairs, LANES, LANES), F32)],
        compiler_params=_params("parallel", "arbitrary", "arbitrary"),
        name="rwkv_scan",
    )(ops, ops, ops, ops, ops, ops, s0)


def _rwkv_post_kernel(of_ref, ob_ref, bonus_ref, gb_ref, g_ref, b_ref, o_ref):
    ones = _head_ones()
    for s in range(o_ref.shape[1] // LANES):
        sl = slice(s * LANES, (s + 1) * LANES)
        o = of_ref[:, sl] + ob_ref[:, sl]
        mu = _head_sum(o, ones) * (1.0 / HEAD_DIM)
        oc = o - mu
        var = _head_sum(oc * oc, ones) * (1.0 / HEAD_DIM)
        o = oc * lax.rsqrt(var + GN_EPS) * g_ref[:, sl] + b_ref[:, sl] + bonus_ref[:, sl]
        o_ref[:, sl] = (o * _silu(gb_ref[:, sl])).astype(BF16)


def _rwkv_post(o2, bonus, z, dh, gn_g, gn_b):
    _, b, l, _ = o2.shape
    tl = _tile(l, 512, SUBLANES)
    tw = _tile(dh, 512, LANES)
    nb = dh // tw
    gate_blk = 7 * nb
    return pl.pallas_call(
        _rwkv_post_kernel,
        grid=(b, l // tl, nb),
        in_specs=[pl.BlockSpec((None, None, tl, tw), lambda bb, i, j: (0, bb, i, j)),
                  pl.BlockSpec((None, None, tl, tw), lambda bb, i, j: (1, bb, i, j)),
                  pl.BlockSpec((None, tl, tw), lambda bb, i, j: (bb, i, j)),
                  pl.BlockSpec((None, tl, tw), lambda bb, i, j: (bb, i, gate_blk + j)),
                  pl.BlockSpec((1, tw), lambda bb, i, j: (0, j)),
                  pl.BlockSpec((1, tw), lambda bb, i, j: (0, j))],
        out_specs=pl.BlockSpec((None, tl, tw), lambda bb, i, j: (bb, i, j)),
        out_shape=jax.ShapeDtypeStruct((b, l, dh), BF16),
        compiler_params=_params("parallel", "parallel", "parallel"),
        name="rwkv_post",
    )(o2, o2, bonus, z, gn_g.reshape(1, dh), gn_b.reshape(1, dh))


def _rwkv_heads(z, dh, pe, s0):
    b, l, _ = z.shape
    nh = dh // HEAD_DIM
    npairs = dh // LANES
    ops, bonus = _rwkv_prep(z, dh, pe["conv_w"], pe["w0"], pe["wup"], pe["a0"], pe["aup"], pe["k_k"], pe["k_a"], pe["r_k"])
    if s0 is None:
        st = jnp.zeros((b, 2, npairs, LANES, LANES), F32)
    else:
        s = s0.astype(F32).reshape(b, 2, npairs, 2, HEAD_DIM, HEAD_DIM)
        zero = jnp.zeros_like(s[:, :, :, 0])
        st = jnp.concatenate([jnp.concatenate([s[:, :, :, 0], zero], axis=-1),
                              jnp.concatenate([zero, s[:, :, :, 1]], axis=-1)], axis=-2)
    o2, s_fin = _rwkv_scan(ops, st)
    s_fin = jnp.stack([s_fin[..., :HEAD_DIM, :HEAD_DIM], s_fin[..., HEAD_DIM:, HEAD_DIM:]], axis=3)
    s_fin = s_fin.reshape(b, 2, nh, HEAD_DIM, HEAD_DIM)
    mixed = _rwkv_post(o2, bonus, z, dh, pe["gn_g"], pe["gn_b"])
    return mixed, s_fin


def _even_layer(x, sc, sh, gt, ln_g, ln_b, alpha, pe, kv_ctx, s0):
    dh = pe["dh"]
    z = _modmm(x, sc, sh, pe["w_in"])
    if kv_ctx is None:
        mixed_a = _ctx_attn(z, dh)
    else:
        mixed_a = _na_attn(z, dh, kv_ctx[0], kv_ctx[1], pe["bias"])
    mixed_b, s_fin = _rwkv_heads(z, dh, pe, s0)
    out = _outln([mixed_a, mixed_b], [pe["w_out"][:dh], pe["w_out"][dh:]], x, gt, ln_g, ln_b, alpha)
    return out, z, s_fin


def _odd_layer(x, sc, sh, gt, ln_g, ln_b, alpha, po, spec, mats):
    c = po["c"]
    b, l, _ = x.shape
    p = _hy_planes(l)
    z = _modmm(x, sc, sh, po["w_in"], planes=p)
    x0, uu, uub = _hy_conv(z, po["conv_w"], po["conv_b"], c)
    mixed = _hy_fftconv(mats, spec, uub, uu, x0, z, po["skip"])
    return _outln([mixed], [po["w_out"]], x, gt, ln_g, ln_b, alpha, planes=p)


def kernel(x_prompt, x_sample, c, cache_a_k, cache_a_v, state_b, c_ctx, w_ada, b_ada, ln_g, ln_b,
           w_in_even, w_out_even, a_rpb, b_conv_w, b_w0, b_wup, b_a0, b_aup, b_kk, b_ka, b_rk,
           b_gn_g, b_gn_b, w_in_odd, w_out_odd, h_conv_w, h_conv_b, h_ffn_w1, h_ffn_b1, h_ffn_w2,
           h_ffn_b2, h_ffn_w3, h_ffn_b3, h_freq, h_ffn_wout, h_skip):
    depth, d, _ = w_ada.shape
    bx, lx, _ = x_prompt.shape
    by, ly, _ = x_sample.shape
    alpha = (2 * depth) ** 0.25
    dh = b_kk.shape[1]
    nh = dh // HEAD_DIM
    p_len = cache_a_k.shape[2]
    c_hy = h_skip.shape[1]

    rows = 1 + by
    rpad = -(-rows // SUBLANES) * SUBLANES
    cond = jnp.pad(jnp.concatenate([c_ctx[None, :], c], axis=0), ((0, rpad - rows), (0, 0)))
    mods = _ada_mod(cond, w_ada, b_ada)

    x = x_prompt
    y = x_sample
    mats_x = _hy_mats(lx) if depth > 1 else None
    mats_y = _hy_mats(ly) if depth > 1 else None
    new_k, new_v, new_s = [], [], []
    for l in range(depth):
        m = mods[l]
        sh_x, sc_x, gt_x = (jnp.broadcast_to(m[0:1, k * d:(k + 1) * d][None], (bx, 1, d)) for k in range(3))
        sh_y, sc_y, gt_y = (m[1:rows, k * d:(k + 1) * d][:, None, :] for k in range(3))
        if l % 2 == 0:
            e = l // 2
            pe = dict(dh=dh, w_in=w_in_even[e].astype(BF16), w_out=w_out_even[e].astype(BF16),
                      bias=_na_bias_table(a_rpb[e]), conv_w=b_conv_w[e], w0=b_w0[e], wup=b_wup[e], a0=b_a0[e],
                      aup=b_aup[e], k_k=b_kk[e], k_a=b_ka[e], r_k=b_rk[e], gn_g=b_gn_g[e], gn_b=b_gn_b[e])
            x, zx, s_fin = _even_layer(x, sc_x, sh_x, gt_x, ln_g[l], ln_b[l], alpha, pe, None, None)
            new_k.append(zx[..., dh:2 * dh].reshape(bx, lx, nh, HEAD_DIM))
            new_v.append(zx[..., 2 * dh:3 * dh].reshape(bx, lx, nh, HEAD_DIM))
            new_s.append(s_fin)
            kv = (cache_a_k[:, e].reshape(by, p_len, dh), cache_a_v[:, e].reshape(by, p_len, dh))
            y, _, _ = _even_layer(y, sc_y, sh_y, gt_y, ln_g[l], ln_b[l], alpha, pe, kv, state_b[:, e])
        else:
            o = l // 2
            po = dict(c=c_hy, w_in=w_in_odd[o].astype(BF16), w_out=w_out_odd[o].astype(BF16),
                      conv_w=h_conv_w[o], conv_b=h_conv_b[o], skip=h_skip[o])
            fargs = (h_ffn_w1[o], h_ffn_b1[o], h_ffn_w2[o], h_ffn_b2[o], h_ffn_w3[o], h_ffn_b3[o],
                     h_freq[o], h_ffn_wout[o])
            spec_x = _hy_spectrum(lx, c_hy, mats_x[0], *fargs)
            spec_y = _hy_spectrum(ly, c_hy, mats_y[0], *fargs)
            x = _odd_layer(x, sc_x, sh_x, gt_x, ln_g[l], ln_b[l], alpha, po, spec_x, mats_x)
            y = _odd_layer(y, sc_y, sh_y, gt_y, ln_g[l], ln_b[l], alpha, po, spec_y, mats_y)
    return (x, y, jnp.stack(new_k, axis=1), jnp.stack(new_v, axis=1), jnp.stack(new_s, axis=1))
```

```python
import functools
import math

import jax
import jax.numpy as jnp
from jax import lax
from jax.experimental import pallas as pl
from jax.experimental.pallas import tpu as pltpu

F32 = jnp.float32
BF16 = jnp.bfloat16

HEAD_DIM = 64
LORA = 64
GRID_W = 64
WIN_H = 8
WIN_W = 16
HY_TARGET = 1e-2
HY_FAST = 0.3
HY_SLOW = 1.5
LN_EPS = 1e-5
GN_EPS = 64e-5
NEG = -1e30

LANES = 128
SUBLANES = 8
VMEM_LIMIT = 56 * 1024 * 1024


def _tile(n, target, mult):
    if n <= target:
        return n
    t = (target // mult) * mult
    while t >= mult:
        if n % t == 0:
            return t
        t -= mult
    return n


def _params(*sem):
    return pltpu.CompilerParams(dimension_semantics=sem, vmem_limit_bytes=VMEM_LIMIT)


def _silu(x):
    return x * jax.nn.sigmoid(x)


def _dot(a, b):
    return jnp.dot(a, b, preferred_element_type=F32)


def _dot_nt(a, b):
    return lax.dot_general(a, b, (((1,), (1,)), ((), ())), preferred_element_type=F32)


def _run_lockstep(gens):
    results = [None] * len(gens)
    active = list(range(len(gens)))
    while active:
        for p in list(active):
            try:
                next(gens[p])
            except StopIteration as done:
                results[p] = done.value
                active.remove(p)
    return results


def _head_ones():
    r = lax.broadcasted_iota(jnp.int32, (LANES, LANES), 0) // HEAD_DIM
    c = lax.broadcasted_iota(jnp.int32, (LANES, LANES), 1) // HEAD_DIM
    return (r == c).astype(BF16)


def _head_sum(x, ones):
    hi = x.astype(BF16)
    r1 = x - hi.astype(F32)
    mid = r1.astype(BF16)
    lo = (r1 - mid.astype(F32)).astype(BF16)
    return _dot(hi, ones) + _dot(mid, ones) + _dot(lo, ones)


def _ada_kernel(c_ref, w_ref, b_ref, o_ref):
    c = c_ref[...]
    o_ref[...] = _dot(_silu(c).astype(BF16), w_ref[...].astype(BF16)) + b_ref[...]


def _ada_mod(cond, w_ada, b_ada):
    depth, d, n = w_ada.shape
    r = cond.shape[0]
    tn = _tile(n, 768, LANES)
    return pl.pallas_call(
        _ada_kernel,
        grid=(depth, n // tn),
        in_specs=[pl.BlockSpec((r, d), lambda l, j: (0, 0)),
                  pl.BlockSpec((None, d, tn), lambda l, j: (l, 0, j)),
                  pl.BlockSpec((None, 1, tn), lambda l, j: (l, 0, j))],
        out_specs=pl.BlockSpec((None, r, tn), lambda l, j: (l, 0, j)),
        out_shape=jax.ShapeDtypeStruct((depth, r, n), F32),
        compiler_params=_params("parallel", "parallel"),
        name="ada_mod",
    )(cond, w_ada, b_ada.reshape(depth, 1, n))


def _modmm_kernel(*refs, nx, planes):
    x_refs = refs[:nx]
    sc_ref, sh_ref, w_ref, o_ref, h_ref = refs[nx:]
    p = planes or 1
    rows = x_refs[0].shape[0] // p
    cols = x_refs[0].shape[1]

    @pl.when(pl.program_id(2) == 0)
    def _():
        for c, x_ref in enumerate(x_refs):
            cs = slice(c * cols, (c + 1) * cols)
            scale = 1.0 + sc_ref[:, cs]
            shift = sh_ref[:, cs]
            for t2 in range(p):
                x = x_ref[...] if p == 1 else x_ref[pl.ds(t2, rows, stride=p), :]
                h_ref[t2 * rows:(t2 + 1) * rows, cs] = (x * scale + shift).astype(BF16)

    out = _dot(h_ref[...], w_ref[...])
    if planes is None:
        o_ref[...] = out
    else:
        for t2 in range(p):
            o_ref[t2] = out[t2 * rows:(t2 + 1) * rows]


def _modmm(x, sc, sh, w, planes=None):
    b, l, d = x.shape
    n = w.shape[1]
    tn = _tile(n, 1024, LANES)
    tm = _tile(l, 1024, SUBLANES)
    if planes is None:
        out_spec = pl.BlockSpec((None, tm, tn), lambda bb, i, j: (bb, i, j))
        out_shape = (b, l, n)
    else:
        assert tm % (planes * 16) == 0
        out_spec = pl.BlockSpec((None, planes, tm // planes, tn), lambda bb, i, j: (bb, 0, i, j))
        out_shape = (b, planes, l // planes, n)
    xw = LANES if (planes or 1) > 1 else d
    nx = d // xw
    return pl.pallas_call(
        functools.partial(_modmm_kernel, nx=nx, planes=planes),
        grid=(b, l // tm, n // tn),
        in_specs=[pl.BlockSpec((None, tm, xw), lambda bb, i, j, c=c: (bb, i, c)) for c in range(nx)]
        + [pl.BlockSpec((None, 1, d), lambda bb, i, j: (bb, 0, 0)),
           pl.BlockSpec((None, 1, d), lambda bb, i, j: (bb, 0, 0)),
           pl.BlockSpec((d, tn), lambda bb, i, j: (0, j))],
        out_specs=out_spec,
        out_shape=jax.ShapeDtypeStruct(out_shape, F32),
        scratch_shapes=[pltpu.VMEM((tm, d), BF16)],
        compiler_params=_params("parallel", "parallel", "arbitrary"),
        name="mod_in_proj",
    )(*([x] * nx), sc, sh, w)


def _outln_kernel(*refs, nparts, alpha):
    m_refs = refs[:nparts]
    w_refs = refs[nparts:2 * nparts]
    x_ref, gt_ref, g_ref, b_ref, o_ref = refs[2 * nparts:]
    acc = _dot(m_refs[0][...], w_refs[0][...])
    for p in range(1, nparts):
        acc = acc + _dot(m_refs[p][...], w_refs[p][...])
    y = alpha * x_ref[...] + gt_ref[...] * acc
    mu = jnp.mean(y, axis=-1, keepdims=True)
    yc = y - mu
    var = jnp.mean(yc * yc, axis=-1, keepdims=True)
    o_ref[...] = yc * lax.rsqrt(var + LN_EPS) * g_ref[...] + b_ref[...]


def _outln(parts, ws, x, gt, ln_g, ln_b, alpha):
    b, l, d = x.shape
    nparts = len(parts)
    tl = _tile(l, 512, SUBLANES)
    row_spec = pl.BlockSpec((None, tl, d), lambda bb, i: (bb, i, 0))
    in_specs = [pl.BlockSpec((None, tl, p.shape[2]), lambda bb, i: (bb, i, 0)) for p in parts]
    in_specs += [pl.BlockSpec(w.shape, lambda bb, i: (0, 0)) for w in ws]
    in_specs += [row_spec,
                 pl.BlockSpec((None, 1, d), lambda bb, i: (bb, 0, 0)),
                 pl.BlockSpec((1, d), lambda bb, i: (0, 0)),
                 pl.BlockSpec((1, d), lambda bb, i: (0, 0))]
    return pl.pallas_call(
        functools.partial(_outln_kernel, nparts=nparts, alpha=alpha),
        grid=(b, l // tl),
        in_specs=in_specs,
        out_specs=row_spec,
        out_shape=jax.ShapeDtypeStruct((b, l, d), F32),
        compiler_params=_params("parallel", "parallel"),
        name="out_proj_ln",
    )(*parts, *ws, x, gt, ln_g.reshape(1, d), ln_b.reshape(1, d))


def _conv_specs(tl, tc, l, colblk):
    g = tl // SUBLANES
    last = l // SUBLANES - 1
    return [
        pl.BlockSpec((None, tl, tc), lambda bb, i, j: (bb, i, colblk + j)),
        pl.BlockSpec((None, SUBLANES, tc), lambda bb, i, j: (bb, jnp.maximum(i * g - 1, 0), colblk + j)),
        pl.BlockSpec((None, SUBLANES, tc), lambda bb, i, j: (bb, jnp.minimum((i + 1) * g, last), colblk + j)),
    ]


def _conv3(x_ref, p_ref, n_ref, w, first, last, lanes=slice(None)):
    x = x_ref[:, lanes]
    tl = x.shape[0]
    row = lax.broadcasted_iota(jnp.int32, x.shape, 0)
    prow = jnp.where(first, 0.0, p_ref[SUBLANES - 1:SUBLANES, lanes])
    nrow = jnp.where(last, 0.0, n_ref[0:1, lanes])
    xm = jnp.where(row == 0, prow, pltpu.roll(x, 1, 0))
    xq = jnp.where(row == tl - 1, nrow, pltpu.roll(x, tl - 1, 0))
    return xm * w[0:1, :] + x * w[1:2, :] + xq * w[2:3, :]


HY_PLANES = 16
HY_PLANE_ROWS = 256


def _hy_planes(l):
    p = HY_PLANES
    while p > 1 and l // p < HY_PLANE_ROWS:
        p //= 2
    return p


def _plane_conv3(z_ref, w, bias):
    p, rows, _ = z_ref.shape
    row = lax.broadcasted_iota(jnp.int32, z_ref.shape[1:], 0)
    before = jnp.where(row == 0, 0.0, pltpu.roll(z_ref[p - 1], 1, 0))
    after = jnp.where(row == rows - 1, 0.0, pltpu.roll(z_ref[0], rows - 1, 0))
    out = []
    for t2 in range(p):
        prev = z_ref[t2 - 1] if t2 > 0 else before
        nxt = z_ref[t2 + 1] if t2 < p - 1 else after
        out.append(prev * w[0:1, :] + z_ref[t2] * w[1:2, :] + nxt * w[2:3, :] + bias)
    return out


def _hy_conv_kernel(z0_ref, z1_ref, z2_ref, w0_ref, w1_ref, w2_ref, b0_ref, b1_ref, b2_ref,
                    x0o_ref, uu_ref, uub_ref):
    x0 = _plane_conv3(z0_ref, w0_ref[...], b0_ref[...])
    x1 = _plane_conv3(z1_ref, w1_ref[...], b1_ref[...])
    v = _plane_conv3(z2_ref, w2_ref[...], b2_ref[...])
    for t2 in range(len(x0)):
        uu = x1[t2] * v[t2]
        x0o_ref[t2] = x0[t2]
        uu_ref[t2] = uu
        uub_ref[t2] = uu.astype(BF16)


def _hy_conv(z, conv_w, conv_b, c):
    b, p, rows, _ = z.shape
    tc = _tile(c, 256, LANES)
    nc = c // tc
    in_specs = [pl.BlockSpec((None, p, rows, tc), lambda bb, j, k=k: (bb, 0, 0, k * nc + j)) for k in range(3)]
    in_specs += [pl.BlockSpec((3, tc), lambda bb, j, k=k: (0, k * nc + j)) for k in range(3)]
    in_specs += [pl.BlockSpec((1, tc), lambda bb, j, k=k: (0, k * nc + j)) for k in range(3)]
    ospec = pl.BlockSpec((None, p, rows, tc), lambda bb, j: (bb, 0, 0, j))
    cb = conv_b.reshape(1, 3 * c)
    return pl.pallas_call(
        _hy_conv_kernel,
        grid=(b, nc),
        in_specs=in_specs,
        out_specs=[ospec, ospec, ospec],
        out_shape=[jax.ShapeDtypeStruct((b, p, rows, c), F32), jax.ShapeDtypeStruct((b, p, rows, c), F32),
                   jax.ShapeDtypeStruct((b, p, rows, c), BF16)],
        compiler_params=_params("parallel", "parallel"),
        name="hyena_gate_conv",
    )(z, z, z, conv_w, conv_w, conv_w, cb, cb, cb)


def _hy_filter_kernel(feats_ref, t_ref, fw1_ref, fb1_ref, fw2_ref, fb2_ref, fw3_ref, fb3_ref, freq_ref,
                      wf_ref, wb_ref, dl_ref, o_ref):
    freq = freq_ref[...]
    hdn = jnp.sin(freq[0:1, :] * (_dot(feats_ref[...].astype(BF16), fw1_ref[...].astype(BF16)) + fb1_ref[...]))
    hdn = jnp.sin(freq[1:2, :] * (_dot(hdn.astype(BF16), fw2_ref[...].astype(BF16)) + fb2_ref[...]))
    hdn = jnp.sin(freq[2:3, :] * (_dot(hdn.astype(BF16), fw3_ref[...].astype(BF16)) + fb3_ref[...]))
    hb16 = hdn.astype(BF16)
    window = jnp.exp(-t_ref[...] * dl_ref[...])
    h_f = _dot(hb16, wf_ref[...].astype(BF16)) * window
    h_b = _dot(hb16, wb_ref[...].astype(BF16)) * window
    h_b = h_b * (t_ref[...] > 0.0).astype(F32)
    norm = jnp.sum(jnp.abs(h_f), axis=0, keepdims=True) + jnp.sum(jnp.abs(h_b), axis=0, keepdims=True)
    inv = 1.0 / norm
    o_ref[0] = (h_f * inv).astype(BF16)
    o_ref[1] = (h_b * inv).astype(BF16)


def _hy_filter(l, c, fw1, fb1, fw2, fb2, fw3, fb3, freq, fwout):
    emb, order = fw1.shape
    bands_n = (emb - 1) // 2
    t = jnp.linspace(0.0, 1.0, l, dtype=F32)[:, None]
    ang = 2.0 * math.pi * jnp.arange(l, dtype=F32)[:, None] / l
    bands = jnp.linspace(1e-4, bands_n - 1, bands_n, dtype=F32)
    feats = jnp.concatenate([t, jnp.cos(ang * bands), -jnp.sin(ang * bands)], axis=-1)
    feats = jnp.pad(feats, ((0, 0), (0, LANES - emb)))
    p = _hy_planes(l)
    plane_major = lambda a: a.reshape(l // p, p, a.shape[1]).transpose(1, 0, 2).reshape(l, a.shape[1])
    feats, t = plane_major(feats), plane_major(t)
    fw1p = jnp.pad(fw1, ((0, LANES - emb), (0, 0)))
    deltas = jnp.abs(jnp.linspace(math.log(HY_TARGET) / HY_SLOW, math.log(HY_TARGET) / HY_FAST, c, dtype=F32))
    tc = _tile(c, 256, LANES)
    nc = c // tc
    full = lambda shape: pl.BlockSpec(shape, lambda j: tuple(0 for _ in shape))
    return pl.pallas_call(
        _hy_filter_kernel,
        grid=(nc,),
        in_specs=[full((l, LANES)), full((l, 1)), full((LANES, order)), full((1, order)),
                  full((order, order)), full((1, order)), full((order, order)), full((1, order)),
                  full((3, order)),
                  pl.BlockSpec((order, tc), lambda j: (0, j)),
                  pl.BlockSpec((order, tc), lambda j: (0, nc + j)),
                  pl.BlockSpec((1, tc), lambda j: (0, j))],
        out_specs=pl.BlockSpec((2, l, tc), lambda j: (0, 0, j)),
        out_shape=jax.ShapeDtypeStruct((2, l, c), BF16),
        compiler_params=_params("parallel"),
        name="hyena_filter",
    )(feats, t, fw1p, fb1.reshape(1, order), fw2, fb2.reshape(1, order), fw3, fb3.reshape(1, order),
      freq, fwout, fwout, deltas.reshape(1, c))


def _hy_mats(l):
    p = _hy_planes(l)
    rows = l // p
    n = 2 * l
    kq = jnp.arange(rows, dtype=jnp.int32)
    t = p * jnp.arange(rows, dtype=jnp.int32)[None, None, :] + jnp.arange(p, dtype=jnp.int32)[:, None, None]
    phase = ((2 * kq[None, :, None] + 1) * t) % (2 * n)
    ang = phase.astype(F32) * (math.pi / n)
    cos, sin = jnp.cos(ang), jnp.sin(ang)
    fwd = jnp.concatenate([cos, -sin], axis=1)
    inv = (2.0 / n) * jnp.concatenate([cos.transpose(0, 2, 1), -sin.transpose(0, 2, 1)], axis=2)
    return fwd.astype(BF16), inv.astype(BF16)


def _plane_butterflies(s_ref, sign, dif):
    p = s_ref.shape[1]
    sizes = [p >> i for i in range(int(math.log2(p)))]
    if not dif:
        sizes = sizes[::-1]
    for size in sizes:
        half = size // 2
        for start in range(0, p, size):
            for k in range(half):
                ang = sign * 2.0 * math.pi * k / size
                wr, wi = math.cos(ang), math.sin(ang)
                i0, i1 = start + k, start + k + half
                ur, ui, vr, vi = s_ref[0, i0], s_ref[1, i0], s_ref[0, i1], s_ref[1, i1]

                def twiddle(xr, xi):
                    if k == 0:
                        return xr, xi
                    if 4 * k == size:
                        return (-xi, xr) if sign > 0 else (xi, -xr)
                    return xr * wr - xi * wi, xr * wi + xi * wr

                if dif:
                    s_ref[0, i0] = ur + vr
                    s_ref[1, i0] = ui + vi
                    tr, ti = twiddle(ur - vr, ui - vi)
                    s_ref[0, i1] = tr
                    s_ref[1, i1] = ti
                else:
                    tr, ti = twiddle(vr, vi)
                    s_ref[0, i0] = ur + tr
                    s_ref[1, i0] = ui + ti
                    s_ref[0, i1] = ur - tr
                    s_ref[1, i1] = ui - ti


def _plane_spectrum(x_ref, f_ref, s_ref):
    p, rows, _ = x_ref.shape
    for t2 in range(p):
        y = _dot(f_ref[t2], x_ref[t2])
        s_ref[0, t2] = y[:rows]
        s_ref[1, t2] = y[rows:]
    _plane_butterflies(s_ref, -1, dif=True)


def _hy_spec_kernel(h_ref, f_ref, o_ref, s_ref):
    p = s_ref.shape[1]
    _plane_spectrum(h_ref.at[0], f_ref, s_ref)
    for i in range(p):
        o_ref[0, i] = s_ref[0, i]
        o_ref[1, i] = s_ref[1, i]
    _plane_spectrum(h_ref.at[1], f_ref, s_ref)
    for i in range(p):
        o_ref[0, i] = o_ref[0, i] + s_ref[0, i]
        o_ref[1, i] = o_ref[1, i] - s_ref[1, i]


def _hy_spectrum(l, c, fwd, fw1, fb1, fw2, fb2, fw3, fb3, freq, fwout):
    p = _hy_planes(l)
    rows = l // p
    hfb = _hy_filter(l, c, fw1, fb1, fw2, fb2, fw3, fb3, freq, fwout).reshape(2, p, rows, c)
    tn = LANES
    return pl.pallas_call(
        _hy_spec_kernel,
        grid=(c // tn,),
        in_specs=[pl.BlockSpec((2, p, rows, tn), lambda j: (0, 0, 0, j)),
                  pl.BlockSpec((p, 2 * rows, rows), lambda j: (0, 0, 0))],
        out_specs=pl.BlockSpec((2, p, rows, tn), lambda j: (0, 0, 0, j)),
        out_shape=jax.ShapeDtypeStruct((2, p, rows, c), F32),
        scratch_shapes=[pltpu.VMEM((2, p, rows, tn), F32)],
        compiler_params=_params("parallel"),
        name="hyena_filter_spectrum",
    )(hfb, fwd)


def _hy_fftconv_kernel(x_ref, f_ref, g_ref, h_ref, uu_ref, x0_ref, gz_ref, skip_ref, o_ref, s_ref):
    p, rows, _ = x_ref.shape
    _plane_spectrum(x_ref, f_ref, s_ref)
    for i in range(p):
        re, im, hre, him = s_ref[0, i], s_ref[1, i], h_ref[0, i], h_ref[1, i]
        s_ref[0, i] = re * hre - im * him
        s_ref[1, i] = re * him + im * hre
    _plane_butterflies(s_ref, 1, dif=False)
    skip = skip_ref[...]
    for t2 in range(p):
        spec = jnp.concatenate([s_ref[0, t2], s_ref[1, t2]], axis=0).astype(BF16)
        y = _dot(g_ref[t2], spec) + uu_ref[t2] * skip
        o_ref[t2] = (_silu(gz_ref[t2]) * (x0_ref[t2] * y)).astype(BF16)


def _hy_fftconv(mats, spec, uub, uu, x0, z, skip):
    b, p, rows, c = uu.shape
    tn = LANES
    nc = c // tn
    tile = pl.BlockSpec((None, p, rows, tn), lambda bb, j: (bb, 0, 0, j))
    return pl.pallas_call(
        _hy_fftconv_kernel,
        grid=(b, nc),
        in_specs=[tile,
                  pl.BlockSpec((p, 2 * rows, rows), lambda bb, j: (0, 0, 0)),
                  pl.BlockSpec((p, rows, 2 * rows), lambda bb, j: (0, 0, 0)),
                  pl.BlockSpec((2, p, rows, tn), lambda bb, j: (0, 0, 0, j)),
                  tile, tile,
                  pl.BlockSpec((None, p, rows, tn), lambda bb, j: (bb, 0, 0, 3 * nc + j)),
                  pl.BlockSpec((1, tn), lambda bb, j: (0, j))],
        out_specs=tile,
        out_shape=jax.ShapeDtypeStruct((b, p, rows, c), BF16),
        scratch_shapes=[pltpu.VMEM((2, p, rows, tn), F32)],
        compiler_params=_params("parallel", "parallel"),
        name="hyena_fftconv",
    )(uub, mats[0], mats[1], spec, uu, x0, z, skip.reshape(1, c))


def _lane_mask(shape):
    return lax.broadcasted_iota(jnp.int32, shape, 1) < HEAD_DIM


def _ctx_attn_kernel(q_ref, k_ref, v_ref, ga_ref, o_ref):
    q = q_ref[...]
    k = k_ref[...].astype(BF16)
    v = v_ref[...].astype(BF16)
    m0 = _lane_mask(q.shape)
    scale = HEAD_DIM ** -0.5
    outs = []
    for h in range(2):
        qh = jnp.where(m0 if h == 0 else jnp.logical_not(m0), q, 0.0).astype(BF16)
        s = _dot_nt(qh, k) * scale
        p = jnp.exp(s - jnp.max(s, axis=-1, keepdims=True))
        den = jnp.sum(p, axis=-1, keepdims=True)
        outs.append(_dot(p.astype(BF16), v) / den)
    o = jnp.where(m0, outs[0], outs[1])
    o_ref[...] = (o * _silu(ga_ref[...])).astype(BF16)


def _ctx_attn(z, dh):
    b, l, _ = z.shape
    nb = dh // LANES
    spec = lambda g: pl.BlockSpec((None, l, LANES), lambda bb, hp, g=g: (bb, 0, g * nb + hp))
    return pl.pallas_call(
        _ctx_attn_kernel,
        grid=(b, nb),
        in_specs=[spec(0), spec(1), spec(2), spec(3)],
        out_specs=pl.BlockSpec((None, l, LANES), lambda bb, hp: (bb, 0, hp)),
        out_shape=jax.ShapeDtypeStruct((b, l, dh), BF16),
        compiler_params=_params("parallel", "parallel"),
        name="ctx_attention",
    )(z, z, z, z)


def _na_bias_table(rpb):
    c = jnp.arange(GRID_W)[:, None]
    kc = jnp.arange(GRID_W)[None, :]
    qstart = jnp.clip(c - WIN_W // 2, 0, GRID_W - WIN_W)
    mask = (kc >= qstart) & (kc < qstart + WIN_W)
    cidx = jnp.clip(kc - c, 1 - WIN_W, WIN_W - 1) + WIN_W - 1
    dr = jnp.arange(WIN_H)[:, None] + jnp.arange(WIN_H)[None, :]
    t = rpb[:, dr][..., cidx]
    t = jnp.where(mask, t, NEG)
    nh = rpb.shape[0]
    t = t.transpose(1, 0, 3, 2, 4).reshape(WIN_H, nh // 2, 2 * GRID_W, WIN_H * GRID_W)
    return t.transpose(1, 0, 2, 3)


NA_ROWS_PER_STEP = 4


def _na_kernel(q_ref, k_ref, v_ref, kc_ref, vc_ref, bias_ref, ga_ref, o_ref, kb_ref, vb_ref, *, rows):
    kb_ref[...] = k_ref[...].astype(BF16)
    vb_ref[...] = v_ref[...].astype(BF16)
    kc = kc_ref[...].astype(BF16)
    vc = vc_ref[...].astype(BF16)
    m0 = _lane_mask((GRID_W, LANES))
    scale = HEAD_DIM ** -0.5
    win = WIN_H * GRID_W

    def one_row(r):
        rs = jnp.clip(r - WIN_H // 2, 0, rows - WIN_H)
        d0 = rs - r + WIN_H - 1
        q0 = pl.multiple_of(r * GRID_W, GRID_W)
        k0 = pl.multiple_of(rs * GRID_W, GRID_W)
        q = q_ref[pl.ds(q0, GRID_W), :]
        q2 = jnp.concatenate([jnp.where(m0, q, 0.0), jnp.where(m0, 0.0, q)], axis=0).astype(BF16)
        kw = kb_ref[pl.ds(k0, win), :]
        vw = vb_ref[pl.ds(k0, win), :]
        s_nb = _dot_nt(q2, kw)
        s_cx = _dot_nt(q2, kc)
        yield
        s_nb = s_nb * scale + bias_ref[d0]
        s_cx = s_cx * scale
        m = jnp.maximum(jnp.max(s_nb, axis=-1, keepdims=True), jnp.max(s_cx, axis=-1, keepdims=True))
        p_nb = jnp.exp(s_nb - m)
        p_cx = jnp.exp(s_cx - m)
        den = jnp.sum(p_nb, axis=-1, keepdims=True) + jnp.sum(p_cx, axis=-1, keepdims=True)
        o2 = _dot(p_nb.astype(BF16), vw) + _dot(p_cx.astype(BF16), vc)
        yield
        o2 = o2 / den
        o = jnp.where(m0, o2[:GRID_W], o2[GRID_W:])
        ga = ga_ref[pl.ds(q0, GRID_W), :]
        o_ref[pl.ds(q0, GRID_W), :] = (o * _silu(ga)).astype(BF16)

    def body(i, carry):
        _run_lockstep([one_row(i * NA_ROWS_PER_STEP + u) for u in range(NA_ROWS_PER_STEP)])
        return carry

    lax.fori_loop(0, rows // NA_ROWS_PER_STEP, body, 0)


def _na_attn(z, dh, k_ctx, v_ctx, bias):
    b, t, _ = z.shape
    rows = t // GRID_W
    assert t % GRID_W == 0 and rows >= WIN_H and rows % NA_ROWS_PER_STEP == 0
    p = k_ctx.shape[1]
    nb = dh // LANES
    spec = lambda g: pl.BlockSpec((None, t, LANES), lambda bb, hp, g=g: (bb, 0, g * nb + hp))
    cspec = pl.BlockSpec((None, p, LANES), lambda bb, hp: (bb, 0, hp))
    return pl.pallas_call(
        functools.partial(_na_kernel, rows=rows),
        grid=(b, nb),
        in_specs=[spec(0), spec(1), spec(2), cspec, cspec,
                  pl.BlockSpec((None, WIN_H, 2 * GRID_W, WIN_H * GRID_W), lambda bb, hp: (hp, 0, 0, 0)),
                  spec(3)],
        out_specs=pl.BlockSpec((None, t, LANES), lambda bb, hp: (bb, 0, hp)),
        out_shape=jax.ShapeDtypeStruct((b, t, dh), BF16),
        scratch_shapes=[pltpu.VMEM((t, LANES), BF16), pltpu.VMEM((t, LANES), BF16)],
        compiler_params=_params("parallel", "parallel"),
        name="na_attention",
    )(z, z, z, k_ctx, v_ctx, bias, z)


N_OPS = 9


def _softplus(x):
    return jnp.maximum(x, 0.0) + jnp.log(1.0 + jnp.exp(-jnp.abs(x)))


def _rwkv_prep_kernel(r_ref, rp_ref, rn_ref, k_ref, kp_ref, kn_ref, v_ref, vp_ref, vn_ref,
                      cwr_ref, cwk_ref, cwv_ref, wdn_ref, adn_ref, wup_ref, aup_ref, w0_ref, a0_ref,
                      kk_ref, ka_ref, rk_ref, o_ref, bonus_ref):
    first = pl.program_id(1) == 0
    last = pl.program_id(1) == pl.num_programs(1) - 1
    ones = _head_ones()
    tw = jnp.tanh(wdn_ref[...]).astype(BF16)
    ad = adn_ref[...].astype(BF16)
    for s in range(bonus_ref.shape[1] // LANES):
        sl = slice(s * LANES, (s + 1) * LANES)
        r = _conv3(r_ref, rp_ref, rn_ref, cwr_ref[:, sl], first, last, sl)
        k = _conv3(k_ref, kp_ref, kn_ref, cwk_ref[:, sl], first, last, sl)
        v = _conv3(v_ref, vp_ref, vn_ref, cwv_ref[:, sl], first, last, sl)
        k_a = ka_ref[:, sl]
        kk = k * kk_ref[:, sl]
        nrm = jnp.sqrt(_head_sum(kk * kk, ones))
        kk = kk / jnp.maximum(nrm, 1e-12)
        o_ref[0, :, sl] = r
        o_ref[1, :, sl] = v
        o_ref[2, :, sl] = kk
        kd_sum = None
        for d in range(2):
            w_raw = -_softplus(-(w0_ref[d:d + 1, sl] + _dot(tw, wup_ref[d, :, sl].astype(BF16)))) - 0.5
            a = jax.nn.sigmoid(a0_ref[d:d + 1, sl] + _dot(ad, aup_ref[d, :, sl].astype(BF16)))
            kd = k * (1.0 + (a - 1.0) * k_a)
            o_ref[3 + d, :, sl] = -jnp.exp(w_raw)
            o_ref[5 + d, :, sl] = kd
            o_ref[7 + d, :, sl] = kk * a
            kd_sum = kd if kd_sum is None else kd_sum + kd
        bonus_ref[:, sl] = _head_sum(r * kd_sum * rk_ref[:, sl], ones) * v


def _rwkv_prep(z, dh, conv_w, w0, wup, a0, aup, k_k, k_a, r_k):
    b, l, _ = z.shape
    tl = _tile(l, 512, SUBLANES)
    tw = _tile(dh, 512, LANES)
    nb = dh // tw
    base = 4 * nb
    lora_blk = 8 * dh // LANES
    in_specs = (_conv_specs(tl, tw, l, base) + _conv_specs(tl, tw, l, base + nb)
                + _conv_specs(tl, tw, l, base + 2 * nb))
    in_specs += [pl.BlockSpec((3, tw), lambda bb, i, j, g=g: (0, g * nb + j)) for g in range(3)]
    in_specs += [pl.BlockSpec((None, tl, LANES), lambda bb, i, j: (bb, i, lora_blk)),
                 pl.BlockSpec((None, tl, LANES), lambda bb, i, j: (bb, i, lora_blk + 1)),
                 pl.BlockSpec((2, LANES, tw), lambda bb, i, j: (0, 0, j)),
                 pl.BlockSpec((2, LANES, tw), lambda bb, i, j: (0, 0, j)),
                 pl.BlockSpec((2, tw), lambda bb, i, j: (0, j)),
                 pl.BlockSpec((2, tw), lambda bb, i, j: (0, j)),
                 pl.BlockSpec((1, tw), lambda bb, i, j: (0, j)),
                 pl.BlockSpec((1, tw), lambda bb, i, j: (0, j)),
                 pl.BlockSpec((1, tw), lambda bb, i, j: (0, j))]
    zeros = jnp.zeros((LORA, dh), F32)
    wup_p = jnp.stack([jnp.concatenate([wup[0], zeros]), jnp.concatenate([zeros, wup[1]])])
    aup_p = jnp.stack([jnp.concatenate([aup[0], zeros]), jnp.concatenate([zeros, aup[1]])])
    return pl.pallas_call(
        _rwkv_prep_kernel,
        grid=(b, l // tl, nb),
        in_specs=in_specs,
        out_specs=[pl.BlockSpec((N_OPS, None, tl, tw), lambda bb, i, j: (0, bb, i, j)),
                   pl.BlockSpec((None, tl, tw), lambda bb, i, j: (bb, i, j))],
        out_shape=[jax.ShapeDtypeStruct((N_OPS, b, l, dh), F32), jax.ShapeDtypeStruct((b, l, dh), F32)],
        compiler_params=_params("parallel", "parallel", "parallel"),
        name="rwkv_prep",
    )(z, z, z, z, z, z, z, z, z, conv_w, conv_w, conv_w, z, z, wup_p, aup_p, w0, a0,
      k_k.reshape(1, dh), k_a.reshape(1, dh), r_k.reshape(1, dh))


SCAN_CHUNK = 64


def _split3(x):
    hi = x.astype(BF16)
    r1 = x - hi.astype(F32)
    mid = r1.astype(BF16)
    return hi, mid, (r1 - mid.astype(F32)).astype(BF16)


def _stack_heads(x, m0):
    return jnp.concatenate([jnp.where(m0, x, 0.0), jnp.where(m0, 0.0, x)], axis=0).astype(BF16)


def _scan_pair(lw, r, v, kk, kd, kb, s2, tri, strict2, incl2, m0, cm0, bd):
    c = lw.shape[0]
    hi, mid, lo = _split3(lw)
    a = _dot(tri, hi) + _dot(tri, mid) + _dot(tri, lo)
    yield
    atot = jnp.sum(lw, axis=0, keepdims=True)
    g_end = jnp.exp(atot - a)
    g_inv = jnp.exp(-a)
    kt = kk * jnp.exp(a - lw)
    rt = r * jnp.exp(a)
    lhs = jnp.concatenate([kt, rt], axis=0).astype(BF16)
    kbar = kd * g_inv
    bbar = kb * g_inv
    rhs = jnp.concatenate([_stack_heads(kbar, m0), _stack_heads(bbar, m0)], axis=0)
    g = _dot_nt(lhs, rhs)
    sh = _dot_nt(lhs, s2.astype(BF16))
    yield
    a_k = jnp.where(strict2, g[:c, :2 * c], 0.0)
    a_b = jnp.where(strict2, g[:c, 2 * c:], 0.0)
    a_rk = jnp.where(incl2, g[c:, :2 * c], 0.0)
    a_rb = jnp.where(incl2, g[c:, 2 * c:], 0.0)
    vs = _stack_heads(v, m0)
    x = sh[:c] + _dot(a_k.astype(BF16), vs)
    o_part = sh[c:] + _dot(a_rk.astype(BF16), vs)
    yield
    n = -a_b
    levels = int(math.log2(c))
    for lvl in range(levels):
        nb16 = n.astype(BF16)
        x = x + _dot(nb16, _stack_heads(x, m0))
        if lvl + 1 < levels:
            nbd = jnp.concatenate([jnp.where(cm0, n, 0.0), jnp.where(cm0, 0.0, n)], axis=0).astype(BF16)
            n = _dot(nb16, nbd)
        yield
    u = x
    o = o_part - _dot(a_rb.astype(BF16), _stack_heads(u, m0))
    w_val = jnp.concatenate([v, u], axis=0).astype(BF16)
    w_key = jnp.concatenate([kd * g_end, -(kb * g_end)], axis=0).astype(BF16)
    upd = lax.dot_general(w_val, w_key, (((0,), (0,)), ((), ())), preferred_element_type=F32)
    yield
    s_new = s2 * jnp.exp(atot) + jnp.where(bd, upd, 0.0)
    return o, s_new


def _scan_kernel(lw_ref, r_ref, v_ref, kk_ref, kd_ref, kb_ref, s0_ref, o_ref, sout_ref, s_ref, *, nchunk, npairs):
    d = pl.program_id(1)
    i = pl.program_id(2)

    @pl.when(i == 0)
    def _():
        s_ref[...] = s0_ref[...]

    c = SCAN_CHUNK
    sign = 1 - 2 * d
    tri = ((lax.broadcasted_iota(jnp.int32, (c, c), 1) - lax.broadcasted_iota(jnp.int32, (c, c), 0)) * sign
           <= 0).astype(BF16)
    diff2 = ((lax.broadcasted_iota(jnp.int32, (c, 2 * c), 1) & (c - 1))
             - lax.broadcasted_iota(jnp.int32, (c, 2 * c), 0)) * sign
    strict2 = diff2 < 0
    incl2 = diff2 <= 0
    m0 = _lane_mask((c, LANES))
    cm0 = lax.broadcasted_iota(jnp.int32, (c, 2 * c), 1) < c
    bd = (lax.broadcasted_iota(jnp.int32, (LANES, LANES), 0) // HEAD_DIM
          == lax.broadcasted_iota(jnp.int32, (LANES, LANES), 1) // HEAD_DIM)
    lanes = [slice(p * LANES, (p + 1) * LANES) for p in range(npairs)]
    results = _run_lockstep([
        _scan_pair(lw_ref[:, sl], r_ref[:, sl], v_ref[:, sl], kk_ref[:, sl], kd_ref[:, sl], kb_ref[:, sl],
                   s_ref[p], tri, strict2, incl2, m0, cm0, bd)
        for p, sl in enumerate(lanes)])
    for p, (o, s_new) in enumerate(results):
        o_ref[:, lanes[p]] = o
        s_ref[p] = s_new

    @pl.when(i == nchunk - 1)
    def _():
        sout_ref[...] = s_ref[...]


def _rwkv_scan(ops, s0):
    _, b, l, dh = ops.shape
    c = SCAN_CHUNK
    assert l % c == 0
    nchunk = l // c
    npairs = dh // LANES
    tb = lambda d, i: i + d * (nchunk - 1 - 2 * i)
    shared = lambda idx: pl.BlockSpec((None, None, c, dh), lambda bb, d, i: (idx, bb, tb(d, i), 0))
    perdir = lambda idx: pl.BlockSpec((None, None, c, dh), lambda bb, d, i: (idx + d, bb, tb(d, i), 0))
    sspec = pl.BlockSpec((None, None, npairs, LANES, LANES), lambda bb, d, i: (bb, d, 0, 0, 0))
    return pl.pallas_call(
        functools.partial(_scan_kernel, nchunk=nchunk, npairs=npairs),
        grid=(b, 2, nchunk),
        in_specs=[perdir(3), shared(0), shared(1), shared(2), perdir(5), perdir(7), sspec],
        out_specs=[pl.BlockSpec((None, None, c, dh), lambda bb, d, i: (d, bb, tb(d, i), 0)), sspec],
        out_shape=[jax.ShapeDtypeStruct((2, b, l, dh), F32),
                   jax.ShapeDtypeStruct((b, 2, npairs, LANES, LANES), F32)],
        scratch_shapes=[pltpu.VMEM((npairs, LANES, LANES), F32)],
        compiler_params=_params("parallel", "arbitrary", "arbitrary"),
        name="rwkv_scan",
    )(ops, ops, ops, ops, ops, ops, s0)


def _rwkv_post_kernel(of_ref, ob_ref, bonus_ref, gb_ref, g_ref, b_ref, o_ref):
    ones = _head_ones()
    for s in range(o_ref.shape[1] // LANES):
        sl = slice(s * LANES, (s + 1) * LANES)
        o = of_ref[:, sl] + ob_ref[:, sl]
        mu = _head_sum(o, ones) * (1.0 / HEAD_DIM)
        oc = o - mu
        var = _head_sum(oc * oc, ones) * (1.0 / HEAD_DIM)
        o = oc * lax.rsqrt(var + GN_EPS) * g_ref[:, sl] + b_ref[:, sl] + bonus_ref[:, sl]
        o_ref[:, sl] = (o * _silu(gb_ref[:, sl])).astype(BF16)


def _rwkv_post(o2, bonus, z, dh, gn_g, gn_b):
    _, b, l, _ = o2.shape
    tl = _tile(l, 512, SUBLANES)
    tw = _tile(dh, 512, LANES)
    nb = dh // tw
    gate_blk = 7 * nb
    return pl.pallas_call(
        _rwkv_post_kernel,
        grid=(b, l // tl, nb),
        in_specs=[pl.BlockSpec((None, None, tl, tw), lambda bb, i, j: (0, bb, i, j)),
                  pl.BlockSpec((None, None, tl, tw), lambda bb, i, j: (1, bb, i, j)),
                  pl.BlockSpec((None, tl, tw), lambda bb, i, j: (bb, i, j)),
                  pl.BlockSpec((None, tl, tw), lambda bb, i, j: (bb, i, gate_blk + j)),
                  pl.BlockSpec((1, tw), lambda bb, i, j: (0, j)),
                  pl.BlockSpec((1, tw), lambda bb, i, j: (0, j))],
        out_specs=pl.BlockSpec((None, tl, tw), lambda bb, i, j: (bb, i, j)),
        out_shape=jax.ShapeDtypeStruct((b, l, dh), BF16),
        compiler_params=_params("parallel", "parallel", "parallel"),
        name="rwkv_post",
    )(o2, o2, bonus, z, gn_g.reshape(1, dh), gn_b.reshape(1, dh))


def _rwkv_heads(z, dh, pe, s0):
    b, l, _ = z.shape
    nh = dh // HEAD_DIM
    npairs = dh // LANES
    ops, bonus = _rwkv_prep(z, dh, pe["conv_w"], pe["w0"], pe["wup"], pe["a0"], pe["aup"], pe["k_k"], pe["k_a"], pe["r_k"])
    if s0 is None:
        st = jnp.zeros((b, 2, npairs, LANES, LANES), F32)
    else:
        s = s0.astype(F32).reshape(b, 2, npairs, 2, HEAD_DIM, HEAD_DIM)
        zero = jnp.zeros_like(s[:, :, :, 0])
        st = jnp.concatenate([jnp.concatenate([s[:, :, :, 0], zero], axis=-1),
                              jnp.concatenate([zero, s[:, :, :, 1]], axis=-1)], axis=-2)
    o2, s_fin = _rwkv_scan(ops, st)
    s_fin = jnp.stack([s_fin[..., :HEAD_DIM, :HEAD_DIM], s_fin[..., HEAD_DIM:, HEAD_DIM:]], axis=3)
    s_fin = s_fin.reshape(b, 2, nh, HEAD_DIM, HEAD_DIM)
    mixed = _rwkv_post(o2, bonus, z, dh, pe["gn_g"], pe["gn_b"])
    return mixed, s_fin


def _even_layer(x, sc, sh, gt, ln_g, ln_b, alpha, pe, kv_ctx, s0):
    dh = pe["dh"]
    z = _modmm(x, sc, sh, pe["w_in"])
    if kv_ctx is None:
        mixed_a = _ctx_attn(z, dh)
    else:
        mixed_a = _na_attn(z, dh, kv_ctx[0], kv_ctx[1], pe["bias"])
    mixed_b, s_fin = _rwkv_heads(z, dh, pe, s0)
    out = _outln([mixed_a, mixed_b], [pe["w_out"][:dh], pe["w_out"][dh:]], x, gt, ln_g, ln_b, alpha)
    return out, z, s_fin


def _odd_layer(x, sc, sh, gt, ln_g, ln_b, alpha, po, spec, mats):
    c = po["c"]
    b, l, _ = x.shape
    p = _hy_planes(l)
    z = _modmm(x, sc, sh, po["w_in"], planes=p)
    x0, uu, uub = _hy_conv(z, po["conv_w"], po["conv_b"], c)
    mixed = _hy_fftconv(mats, spec, uub, uu, x0, z, po["skip"])
    mixed = mixed.transpose(0, 2, 1, 3).reshape(b, l, c)
    return _outln([mixed], [po["w_out"]], x, gt, ln_g, ln_b, alpha)


def kernel(x_prompt, x_sample, c, cache_a_k, cache_a_v, state_b, c_ctx, w_ada, b_ada, ln_g, ln_b,
           w_in_even, w_out_even, a_rpb, b_conv_w, b_w0, b_wup, b_a0, b_aup, b_kk, b_ka, b_rk,
           b_gn_g, b_gn_b, w_in_odd, w_out_odd, h_conv_w, h_conv_b, h_ffn_w1, h_ffn_b1, h_ffn_w2,
           h_ffn_b2, h_ffn_w3, h_ffn_b3, h_freq, h_ffn_wout, h_skip):
    depth, d, _ = w_ada.shape
    bx, lx, _ = x_prompt.shape
    by, ly, _ = x_sample.shape
    alpha = (2 * depth) ** 0.25
    dh = b_kk.shape[1]
    nh = dh // HEAD_DIM
    p_len = cache_a_k.shape[2]
    c_hy = h_skip.shape[1]

    rows = 1 + by
    rpad = -(-rows // SUBLANES) * SUBLANES
    cond = jnp.pad(jnp.concatenate([c_ctx[None, :], c], axis=0), ((0, rpad - rows), (0, 0)))
    mods = _ada_mod(cond, w_ada, b_ada)

    x = x_prompt
    y = x_sample
    mats_x = _hy_mats(lx) if depth > 1 else None
    mats_y = _hy_mats(ly) if depth > 1 else None
    new_k, new_v, new_s = [], [], []
    for l in range(depth):
        m = mods[l]
        sh_x, sc_x, gt_x = (jnp.broadcast_to(m[0:1, k * d:(k + 1) * d][None], (bx, 1, d)) for k in range(3))
        sh_y, sc_y, gt_y = (m[1:rows, k * d:(k + 1) * d][:, None, :] for k in range(3))
        if l % 2 == 0:
            e = l // 2
            pe = dict(dh=dh, w_in=w_in_even[e].astype(BF16), w_out=w_out_even[e].astype(BF16),
                      bias=_na_bias_table(a_rpb[e]), conv_w=b_conv_w[e], w0=b_w0[e], wup=b_wup[e], a0=b_a0[e],
                      aup=b_aup[e], k_k=b_kk[e], k_a=b_ka[e], r_k=b_rk[e], gn_g=b_gn_g[e], gn_b=b_gn_b[e])
            x, zx, s_fin = _even_layer(x, sc_x, sh_x, gt_x, ln_g[l], ln_b[l], alpha, pe, None, None)
            new_k.append(zx[..., dh:2 * dh].reshape(bx, lx, nh, HEAD_DIM))
            new_v.append(zx[..., 2 * dh:3 * dh].reshape(bx, lx, nh, HEAD_DIM))
            new_s.append(s_fin)
            kv = (cache_a_k[:, e].reshape(by, p_len, dh), cache_a_v[:, e].reshape(by, p_len, dh))
            y, _, _ = _even_layer(y, sc_y, sh_y, gt_y, ln_g[l], ln_b[l], alpha, pe, kv, state_b[:, e])
        else:
            o = l // 2
            po = dict(c=c_hy, w_in=w_in_odd[o].astype(BF16), w_out=w_out_odd[o].astype(BF16),
                      conv_w=h_conv_w[o], conv_b=h_conv_b[o], skip=h_skip[o])
            fargs = (h_ffn_w1[o], h_ffn_b1[o], h_ffn_w2[o], h_ffn_b2[o], h_ffn_w3[o], h_ffn_b3[o],
                     h_freq[o], h_ffn_wout[o])
            spec_x = _hy_spectrum(lx, c_hy, mats_x[0], *fargs)
            spec_y = _hy_spectrum(ly, c_hy, mats_y[0], *fargs)
            x = _odd_layer(x, sc_x, sh_x, gt_x, ln_g[l], ln_b[l], alpha, po, spec_x, mats_x)
            y = _odd_layer(y, sc_y, sh_y, gt_y, ln_g[l], ln_b[l], alpha, po, spec_y, mats_y)
    return (x, y, jnp.stack(new_k, axis=1), jnp.stack(new_v, axis=1), jnp.stack(new_s, axis=1))
```

```python
import functools
import math

import jax
import jax.numpy as jnp
from jax import lax
from jax.experimental import pallas as pl
from jax.experimental.pallas import tpu as pltpu

F32 = jnp.float32
BF16 = jnp.bfloat16

HEAD_DIM = 64
LORA = 64
GRID_W = 64
WIN_H = 8
WIN_W = 16
HY_TARGET = 1e-2
HY_FAST = 0.3
HY_SLOW = 1.5
LN_EPS = 1e-5
GN_EPS = 64e-5
NEG = -1e30

LANES = 128
SUBLANES = 8
VMEM_LIMIT = 56 * 1024 * 1024


def _tile(n, target, mult):
    if n <= target:
        return n
    t = (target // mult) * mult
    while t >= mult:
        if n % t == 0:
            return t
        t -= mult
    return n


def _params(*sem):
    return pltpu.CompilerParams(dimension_semantics=sem, vmem_limit_bytes=VMEM_LIMIT)


def _silu(x):
    return x * jax.nn.sigmoid(x)


def _dot(a, b):
    return jnp.dot(a, b, preferred_element_type=F32)


def _dot_nt(a, b):
    return lax.dot_general(a, b, (((1,), (1,)), ((), ())), preferred_element_type=F32)


def _run_lockstep(gens):
    results = [None] * len(gens)
    active = list(range(len(gens)))
    while active:
        for p in list(active):
            try:
                next(gens[p])
            except StopIteration as done:
                results[p] = done.value
                active.remove(p)
    return results


def _head_ones():
    r = lax.broadcasted_iota(jnp.int32, (LANES, LANES), 0) // HEAD_DIM
    c = lax.broadcasted_iota(jnp.int32, (LANES, LANES), 1) // HEAD_DIM
    return (r == c).astype(BF16)


def _head_sum(x, ones):
    hi = x.astype(BF16)
    r1 = x - hi.astype(F32)
    mid = r1.astype(BF16)
    lo = (r1 - mid.astype(F32)).astype(BF16)
    return _dot(hi, ones) + _dot(mid, ones) + _dot(lo, ones)


def _ada_kernel(c_ref, w_ref, b_ref, o_ref):
    c = c_ref[...]
    o_ref[...] = _dot(_silu(c).astype(BF16), w_ref[...].astype(BF16)) + b_ref[...]


def _ada_mod(cond, w_ada, b_ada):
    depth, d, n = w_ada.shape
    r = cond.shape[0]
    tn = _tile(n, 768, LANES)
    return pl.pallas_call(
        _ada_kernel,
        grid=(depth, n // tn),
        in_specs=[pl.BlockSpec((r, d), lambda l, j: (0, 0)),
                  pl.BlockSpec((None, d, tn), lambda l, j: (l, 0, j)),
                  pl.BlockSpec((None, 1, tn), lambda l, j: (l, 0, j))],
        out_specs=pl.BlockSpec((None, r, tn), lambda l, j: (l, 0, j)),
        out_shape=jax.ShapeDtypeStruct((depth, r, n), F32),
        compiler_params=_params("parallel", "parallel"),
        name="ada_mod",
    )(cond, w_ada, b_ada.reshape(depth, 1, n))


def _modmm_kernel(*refs, nx, planes):
    x_refs = refs[:nx]
    sc_ref, sh_ref, w_ref, o_ref, h_ref = refs[nx:]
    p = planes or 1
    rows = x_refs[0].shape[0] // p
    cols = x_refs[0].shape[1]

    @pl.when(pl.program_id(2) == 0)
    def _():
        for c, x_ref in enumerate(x_refs):
            cs = slice(c * cols, (c + 1) * cols)
            scale = 1.0 + sc_ref[:, cs]
            shift = sh_ref[:, cs]
            for t2 in range(p):
                x = x_ref[...] if p == 1 else x_ref[pl.ds(t2, rows, stride=p), :]
                h_ref[t2 * rows:(t2 + 1) * rows, cs] = (x * scale + shift).astype(BF16)

    out = _dot(h_ref[...], w_ref[...])
    if planes is None:
        o_ref[...] = out
    else:
        for t2 in range(p):
            o_ref[t2] = out[t2 * rows:(t2 + 1) * rows]


def _modmm(x, sc, sh, w, planes=None):
    b, l, d = x.shape
    n = w.shape[1]
    tn = _tile(n, 1024, LANES)
    tm = _tile(l, 1024, SUBLANES)
    if planes is None:
        out_spec = pl.BlockSpec((None, tm, tn), lambda bb, i, j: (bb, i, j))
        out_shape = (b, l, n)
    else:
        assert tm % (planes * 16) == 0
        out_spec = pl.BlockSpec((None, planes, tm // planes, tn), lambda bb, i, j: (bb, 0, i, j))
        out_shape = (b, planes, l // planes, n)
    xw = LANES if (planes or 1) > 1 else d
    nx = d // xw
    return pl.pallas_call(
        functools.partial(_modmm_kernel, nx=nx, planes=planes),
        grid=(b, l // tm, n // tn),
        in_specs=[pl.BlockSpec((None, tm, xw), lambda bb, i, j, c=c: (bb, i, c)) for c in range(nx)]
        + [pl.BlockSpec((None, 1, d), lambda bb, i, j: (bb, 0, 0)),
           pl.BlockSpec((None, 1, d), lambda bb, i, j: (bb, 0, 0)),
           pl.BlockSpec((d, tn), lambda bb, i, j: (0, j))],
        out_specs=out_spec,
        out_shape=jax.ShapeDtypeStruct(out_shape, F32),
        scratch_shapes=[pltpu.VMEM((tm, d), BF16)],
        compiler_params=_params("parallel", "parallel", "arbitrary"),
        name="mod_in_proj",
    )(*([x] * nx), sc, sh, w)


def _outln_kernel(*refs, nparts, alpha):
    m_refs = refs[:nparts]
    w_refs = refs[nparts:2 * nparts]
    x_ref, gt_ref, g_ref, b_ref, o_ref = refs[2 * nparts:]
    acc = _dot(m_refs[0][...], w_refs[0][...])
    for p in range(1, nparts):
        acc = acc + _dot(m_refs[p][...], w_refs[p][...])
    y = alpha * x_ref[...] + gt_ref[...] * acc
    mu = jnp.mean(y, axis=-1, keepdims=True)
    yc = y - mu
    var = jnp.mean(yc * yc, axis=-1, keepdims=True)
    o_ref[...] = yc * lax.rsqrt(var + LN_EPS) * g_ref[...] + b_ref[...]


def _outln(parts, ws, x, gt, ln_g, ln_b, alpha):
    b, l, d = x.shape
    nparts = len(parts)
    tl = _tile(l, 512, SUBLANES)
    row_spec = pl.BlockSpec((None, tl, d), lambda bb, i: (bb, i, 0))
    in_specs = [pl.BlockSpec((None, tl, p.shape[2]), lambda bb, i: (bb, i, 0)) for p in parts]
    in_specs += [pl.BlockSpec(w.shape, lambda bb, i: (0, 0)) for w in ws]
    in_specs += [row_spec,
                 pl.BlockSpec((None, 1, d), lambda bb, i: (bb, 0, 0)),
                 pl.BlockSpec((1, d), lambda bb, i: (0, 0)),
                 pl.BlockSpec((1, d), lambda bb, i: (0, 0))]
    return pl.pallas_call(
        functools.partial(_outln_kernel, nparts=nparts, alpha=alpha),
        grid=(b, l // tl),
        in_specs=in_specs,
        out_specs=row_spec,
        out_shape=jax.ShapeDtypeStruct((b, l, d), F32),
        compiler_params=_params("parallel", "parallel"),
        name="out_proj_ln",
    )(*parts, *ws, x, gt, ln_g.reshape(1, d), ln_b.reshape(1, d))


def _conv_specs(tl, tc, l, colblk):
    g = tl // SUBLANES
    last = l // SUBLANES - 1
    return [
        pl.BlockSpec((None, tl, tc), lambda bb, i, j: (bb, i, colblk + j)),
        pl.BlockSpec((None, SUBLANES, tc), lambda bb, i, j: (bb, jnp.maximum(i * g - 1, 0), colblk + j)),
        pl.BlockSpec((None, SUBLANES, tc), lambda bb, i, j: (bb, jnp.minimum((i + 1) * g, last), colblk + j)),
    ]


def _conv3(x_ref, p_ref, n_ref, w, first, last, lanes=slice(None)):
    x = x_ref[:, lanes]
    tl = x.shape[0]
    row = lax.broadcasted_iota(jnp.int32, x.shape, 0)
    prow = jnp.where(first, 0.0, p_ref[SUBLANES - 1:SUBLANES, lanes])
    nrow = jnp.where(last, 0.0, n_ref[0:1, lanes])
    xm = jnp.where(row == 0, prow, pltpu.roll(x, 1, 0))
    xq = jnp.where(row == tl - 1, nrow, pltpu.roll(x, tl - 1, 0))
    return xm * w[0:1, :] + x * w[1:2, :] + xq * w[2:3, :]


HY_PLANES = 16
HY_PLANE_ROWS = 256


def _hy_planes(l):
    p = HY_PLANES
    while p > 1 and l // p < HY_PLANE_ROWS:
        p //= 2
    return p


def _hy_cols(p):
    return LANES * (HY_PLANES // p)


def _plane_conv3(z_ref, w, bias):
    p, rows, _ = z_ref.shape
    row = lax.broadcasted_iota(jnp.int32, z_ref.shape[1:], 0)
    before = jnp.where(row == 0, 0.0, pltpu.roll(z_ref[p - 1], 1, 0))
    after = jnp.where(row == rows - 1, 0.0, pltpu.roll(z_ref[0], rows - 1, 0))
    out = []
    for t2 in range(p):
        prev = z_ref[t2 - 1] if t2 > 0 else before
        nxt = z_ref[t2 + 1] if t2 < p - 1 else after
        out.append(prev * w[0:1, :] + z_ref[t2] * w[1:2, :] + nxt * w[2:3, :] + bias)
    return out


def _hy_conv_kernel(z0_ref, z1_ref, z2_ref, w0_ref, w1_ref, w2_ref, b0_ref, b1_ref, b2_ref,
                    x0o_ref, uu_ref, uub_ref):
    x0 = _plane_conv3(z0_ref, w0_ref[...], b0_ref[...])
    x1 = _plane_conv3(z1_ref, w1_ref[...], b1_ref[...])
    v = _plane_conv3(z2_ref, w2_ref[...], b2_ref[...])
    for t2 in range(len(x0)):
        uu = x1[t2] * v[t2]
        x0o_ref[t2] = x0[t2]
        uu_ref[t2] = uu
        uub_ref[t2] = uu.astype(BF16)


def _hy_conv(z, conv_w, conv_b, c):
    b, p, rows, _ = z.shape
    tc = _tile(c, 2 * _hy_cols(p), LANES)
    nc = c // tc
    in_specs = [pl.BlockSpec((None, p, rows, tc), lambda bb, j, k=k: (bb, 0, 0, k * nc + j)) for k in range(3)]
    in_specs += [pl.BlockSpec((3, tc), lambda bb, j, k=k: (0, k * nc + j)) for k in range(3)]
    in_specs += [pl.BlockSpec((1, tc), lambda bb, j, k=k: (0, k * nc + j)) for k in range(3)]
    ospec = pl.BlockSpec((None, p, rows, tc), lambda bb, j: (bb, 0, 0, j))
    cb = conv_b.reshape(1, 3 * c)
    return pl.pallas_call(
        _hy_conv_kernel,
        grid=(b, nc),
        in_specs=in_specs,
        out_specs=[ospec, ospec, ospec],
        out_shape=[jax.ShapeDtypeStruct((b, p, rows, c), F32), jax.ShapeDtypeStruct((b, p, rows, c), F32),
                   jax.ShapeDtypeStruct((b, p, rows, c), BF16)],
        compiler_params=_params("parallel", "parallel"),
        name="hyena_gate_conv",
    )(z, z, z, conv_w, conv_w, conv_w, cb, cb, cb)


def _hy_filter_kernel(feats_ref, t_ref, fw1_ref, fb1_ref, fw2_ref, fb2_ref, fw3_ref, fb3_ref, freq_ref,
                      wf_ref, wb_ref, dl_ref, o_ref):
    freq = freq_ref[...]
    hdn = jnp.sin(freq[0:1, :] * (_dot(feats_ref[...].astype(BF16), fw1_ref[...].astype(BF16)) + fb1_ref[...]))
    hdn = jnp.sin(freq[1:2, :] * (_dot(hdn.astype(BF16), fw2_ref[...].astype(BF16)) + fb2_ref[...]))
    hdn = jnp.sin(freq[2:3, :] * (_dot(hdn.astype(BF16), fw3_ref[...].astype(BF16)) + fb3_ref[...]))
    hb16 = hdn.astype(BF16)
    window = jnp.exp(-t_ref[...] * dl_ref[...])
    h_f = _dot(hb16, wf_ref[...].astype(BF16)) * window
    h_b = _dot(hb16, wb_ref[...].astype(BF16)) * window
    h_b = h_b * (t_ref[...] > 0.0).astype(F32)
    norm = jnp.sum(jnp.abs(h_f), axis=0, keepdims=True) + jnp.sum(jnp.abs(h_b), axis=0, keepdims=True)
    inv = 1.0 / norm
    o_ref[0] = (h_f * inv).astype(BF16)
    o_ref[1] = (h_b * inv).astype(BF16)


def _hy_filter(l, c, fw1, fb1, fw2, fb2, fw3, fb3, freq, fwout):
    emb, order = fw1.shape
    bands_n = (emb - 1) // 2
    t = jnp.linspace(0.0, 1.0, l, dtype=F32)[:, None]
    ang = 2.0 * math.pi * jnp.arange(l, dtype=F32)[:, None] / l
    bands = jnp.linspace(1e-4, bands_n - 1, bands_n, dtype=F32)
    feats = jnp.concatenate([t, jnp.cos(ang * bands), -jnp.sin(ang * bands)], axis=-1)
    feats = jnp.pad(feats, ((0, 0), (0, LANES - emb)))
    p = _hy_planes(l)
    plane_major = lambda a: a.reshape(l // p, p, a.shape[1]).transpose(1, 0, 2).reshape(l, a.shape[1])
    feats, t = plane_major(feats), plane_major(t)
    fw1p = jnp.pad(fw1, ((0, LANES - emb), (0, 0)))
    deltas = jnp.abs(jnp.linspace(math.log(HY_TARGET) / HY_SLOW, math.log(HY_TARGET) / HY_FAST, c, dtype=F32))
    tc = _tile(c, 256, LANES)
    nc = c // tc
    full = lambda shape: pl.BlockSpec(shape, lambda j: tuple(0 for _ in shape))
    return pl.pallas_call(
        _hy_filter_kernel,
        grid=(nc,),
        in_specs=[full((l, LANES)), full((l, 1)), full((LANES, order)), full((1, order)),
                  full((order, order)), full((1, order)), full((order, order)), full((1, order)),
                  full((3, order)),
                  pl.BlockSpec((order, tc), lambda j: (0, j)),
                  pl.BlockSpec((order, tc), lambda j: (0, nc + j)),
                  pl.BlockSpec((1, tc), lambda j: (0, j))],
        out_specs=pl.BlockSpec((2, l, tc), lambda j: (0, 0, j)),
        out_shape=jax.ShapeDtypeStruct((2, l, c), BF16),
        compiler_params=_params("parallel"),
        name="hyena_filter",
    )(feats, t, fw1p, fb1.reshape(1, order), fw2, fb2.reshape(1, order), fw3, fb3.reshape(1, order),
      freq, fwout, fwout, deltas.reshape(1, c))


def _hy_mats(l):
    p = _hy_planes(l)
    rows = l // p
    n = 2 * l
    kq = jnp.arange(rows, dtype=jnp.int32)
    t = p * jnp.arange(rows, dtype=jnp.int32)[None, None, :] + jnp.arange(p, dtype=jnp.int32)[:, None, None]
    phase = ((2 * kq[None, :, None] + 1) * t) % (2 * n)
    ang = phase.astype(F32) * (math.pi / n)
    cos, sin = jnp.cos(ang), jnp.sin(ang)
    fwd = jnp.concatenate([cos, -sin], axis=1)
    inv = (2.0 / n) * jnp.concatenate([cos.transpose(0, 2, 1), -sin.transpose(0, 2, 1)], axis=2)
    return fwd.astype(BF16), inv.astype(BF16)


def _plane_butterflies(s_ref, sign, dif):
    p = s_ref.shape[1]
    sizes = [p >> i for i in range(int(math.log2(p)))]
    if not dif:
        sizes = sizes[::-1]
    for size in sizes:
        half = size // 2
        for start in range(0, p, size):
            for k in range(half):
                ang = sign * 2.0 * math.pi * k / size
                wr, wi = math.cos(ang), math.sin(ang)
                i0, i1 = start + k, start + k + half
                ur, ui, vr, vi = s_ref[0, i0], s_ref[1, i0], s_ref[0, i1], s_ref[1, i1]

                def twiddle(xr, xi):
                    if k == 0:
                        return xr, xi
                    if 4 * k == size:
                        return (-xi, xr) if sign > 0 else (xi, -xr)
                    return xr * wr - xi * wi, xr * wi + xi * wr

                if dif:
                    s_ref[0, i0] = ur + vr
                    s_ref[1, i0] = ui + vi
                    tr, ti = twiddle(ur - vr, ui - vi)
                    s_ref[0, i1] = tr
                    s_ref[1, i1] = ti
                else:
                    tr, ti = twiddle(vr, vi)
                    s_ref[0, i0] = ur + tr
                    s_ref[1, i0] = ui + ti
                    s_ref[0, i1] = ur - tr
                    s_ref[1, i1] = ui - ti


def _plane_spectrum(x_ref, f_ref, s_ref):
    p, rows, _ = x_ref.shape
    for t2 in range(p):
        y = _dot(f_ref[t2], x_ref[t2])
        s_ref[0, t2] = y[:rows]
        s_ref[1, t2] = y[rows:]
    _plane_butterflies(s_ref, -1, dif=True)


def _hy_spec_kernel(h_ref, f_ref, o_ref, s_ref):
    p = s_ref.shape[1]
    _plane_spectrum(h_ref.at[0], f_ref, s_ref)
    for i in range(p):
        o_ref[0, i] = s_ref[0, i]
        o_ref[1, i] = s_ref[1, i]
    _plane_spectrum(h_ref.at[1], f_ref, s_ref)
    for i in range(p):
        o_ref[0, i] = o_ref[0, i] + s_ref[0, i]
        o_ref[1, i] = o_ref[1, i] - s_ref[1, i]


def _hy_spectrum(l, c, fwd, fw1, fb1, fw2, fb2, fw3, fb3, freq, fwout):
    p = _hy_planes(l)
    rows = l // p
    hfb = _hy_filter(l, c, fw1, fb1, fw2, fb2, fw3, fb3, freq, fwout).reshape(2, p, rows, c)
    tn = _tile(c, _hy_cols(p), LANES)
    return pl.pallas_call(
        _hy_spec_kernel,
        grid=(c // tn,),
        in_specs=[pl.BlockSpec((2, p, rows, tn), lambda j: (0, 0, 0, j)),
                  pl.BlockSpec((p, 2 * rows, rows), lambda j: (0, 0, 0))],
        out_specs=pl.BlockSpec((2, p, rows, tn), lambda j: (0, 0, 0, j)),
        out_shape=jax.ShapeDtypeStruct((2, p, rows, c), F32),
        scratch_shapes=[pltpu.VMEM((2, p, rows, tn), F32)],
        compiler_params=_params("parallel"),
        name="hyena_filter_spectrum",
    )(hfb, fwd)


def _hy_fftconv_kernel(x_ref, f_ref, g_ref, h_ref, uu_ref, x0_ref, gz_ref, skip_ref, o_ref, s_ref):
    p, rows, _ = x_ref.shape
    _plane_spectrum(x_ref, f_ref, s_ref)
    for i in range(p):
        re, im, hre, him = s_ref[0, i], s_ref[1, i], h_ref[0, i], h_ref[1, i]
        s_ref[0, i] = re * hre - im * him
        s_ref[1, i] = re * him + im * hre
    _plane_butterflies(s_ref, 1, dif=False)
    skip = skip_ref[...]
    for t2 in range(p):
        spec = jnp.concatenate([s_ref[0, t2], s_ref[1, t2]], axis=0).astype(BF16)
        y = _dot(g_ref[t2], spec) + uu_ref[t2] * skip
        o_ref[t2] = (_silu(gz_ref[t2]) * (x0_ref[t2] * y)).astype(BF16)


def _hy_fftconv(mats, spec, uub, uu, x0, z, skip):
    b, p, rows, c = uu.shape
    tn = _tile(c, _hy_cols(p), LANES)
    nc = c // tn
    tile = pl.BlockSpec((None, p, rows, tn), lambda bb, j: (bb, 0, 0, j))
    return pl.pallas_call(
        _hy_fftconv_kernel,
        grid=(b, nc),
        in_specs=[tile,
                  pl.BlockSpec((p, 2 * rows, rows), lambda bb, j: (0, 0, 0)),
                  pl.BlockSpec((p, rows, 2 * rows), lambda bb, j: (0, 0, 0)),
                  pl.BlockSpec((2, p, rows, tn), lambda bb, j: (0, 0, 0, j)),
                  tile, tile,
                  pl.BlockSpec((None, p, rows, tn), lambda bb, j: (bb, 0, 0, 3 * nc + j)),
                  pl.BlockSpec((1, tn), lambda bb, j: (0, j))],
        out_specs=tile,
        out_shape=jax.ShapeDtypeStruct((b, p, rows, c), BF16),
        scratch_shapes=[pltpu.VMEM((2, p, rows, tn), F32)],
        compiler_params=_params("parallel", "parallel"),
        name="hyena_fftconv",
    )(uub, mats[0], mats[1], spec, uu, x0, z, skip.reshape(1, c))


def _lane_mask(shape):
    return lax.broadcasted_iota(jnp.int32, shape, 1) < HEAD_DIM


def _ctx_attn_kernel(q_ref, k_ref, v_ref, ga_ref, o_ref):
    q = q_ref[...]
    k = k_ref[...].astype(BF16)
    v = v_ref[...].astype(BF16)
    m0 = _lane_mask(q.shape)
    scale = HEAD_DIM ** -0.5
    outs = []
    for h in range(2):
        qh = jnp.where(m0 if h == 0 else jnp.logical_not(m0), q, 0.0).astype(BF16)
        s = _dot_nt(qh, k) * scale
        p = jnp.exp(s - jnp.max(s, axis=-1, keepdims=True))
        den = jnp.sum(p, axis=-1, keepdims=True)
        outs.append(_dot(p.astype(BF16), v) / den)
    o = jnp.where(m0, outs[0], outs[1])
    o_ref[...] = (o * _silu(ga_ref[...])).astype(BF16)


def _ctx_attn(z, dh):
    b, l, _ = z.shape
    nb = dh // LANES
    spec = lambda g: pl.BlockSpec((None, l, LANES), lambda bb, hp, g=g: (bb, 0, g * nb + hp))
    return pl.pallas_call(
        _ctx_attn_kernel,
        grid=(b, nb),
        in_specs=[spec(0), spec(1), spec(2), spec(3)],
        out_specs=pl.BlockSpec((None, l, LANES), lambda bb, hp: (bb, 0, hp)),
        out_shape=jax.ShapeDtypeStruct((b, l, dh), BF16),
        compiler_params=_params("parallel", "parallel"),
        name="ctx_attention",
    )(z, z, z, z)


def _na_bias_table(rpb):
    c = jnp.arange(GRID_W)[:, None]
    kc = jnp.arange(GRID_W)[None, :]
    qstart = jnp.clip(c - WIN_W // 2, 0, GRID_W - WIN_W)
    mask = (kc >= qstart) & (kc < qstart + WIN_W)
    cidx = jnp.clip(kc - c, 1 - WIN_W, WIN_W - 1) + WIN_W - 1
    dr = jnp.arange(WIN_H)[:, None] + jnp.arange(WIN_H)[None, :]
    t = rpb[:, dr][..., cidx]
    t = jnp.where(mask, t, NEG)
    nh = rpb.shape[0]
    t = t.transpose(1, 0, 3, 2, 4).reshape(WIN_H, nh // 2, 2 * GRID_W, WIN_H * GRID_W)
    return t.transpose(1, 0, 2, 3)


NA_ROWS_PER_STEP = 4


def _na_kernel(q_ref, k_ref, v_ref, kc_ref, vc_ref, bias_ref, ga_ref, o_ref, kb_ref, vb_ref, *, rows):
    kb_ref[...] = k_ref[...].astype(BF16)
    vb_ref[...] = v_ref[...].astype(BF16)
    kc = kc_ref[...].astype(BF16)
    vc = vc_ref[...].astype(BF16)
    m0 = _lane_mask((GRID_W, LANES))
    scale = HEAD_DIM ** -0.5
    win = WIN_H * GRID_W

    def one_row(r):
        rs = jnp.clip(r - WIN_H // 2, 0, rows - WIN_H)
        d0 = rs - r + WIN_H - 1
        q0 = pl.multiple_of(r * GRID_W, GRID_W)
        k0 = pl.multiple_of(rs * GRID_W, GRID_W)
        q = q_ref[pl.ds(q0, GRID_W), :]
        q2 = jnp.concatenate([jnp.where(m0, q, 0.0), jnp.where(m0, 0.0, q)], axis=0).astype(BF16)
        kw = kb_ref[pl.ds(k0, win), :]
        vw = vb_ref[pl.ds(k0, win), :]
        s_nb = _dot_nt(q2, kw)
        s_cx = _dot_nt(q2, kc)
        yield
        s_nb = s_nb * scale + bias_ref[d0]
        s_cx = s_cx * scale
        m = jnp.maximum(jnp.max(s_nb, axis=-1, keepdims=True), jnp.max(s_cx, axis=-1, keepdims=True))
        p_nb = jnp.exp(s_nb - m)
        p_cx = jnp.exp(s_cx - m)
        den = jnp.sum(p_nb, axis=-1, keepdims=True) + jnp.sum(p_cx, axis=-1, keepdims=True)
        o2 = _dot(p_nb.astype(BF16), vw) + _dot(p_cx.astype(BF16), vc)
        yield
        o2 = o2 / den
        o = jnp.where(m0, o2[:GRID_W], o2[GRID_W:])
        ga = ga_ref[pl.ds(q0, GRID_W), :]
        o_ref[pl.ds(q0, GRID_W), :] = (o * _silu(ga)).astype(BF16)

    def body(i, carry):
        _run_lockstep([one_row(i * NA_ROWS_PER_STEP + u) for u in range(NA_ROWS_PER_STEP)])
        return carry

    lax.fori_loop(0, rows // NA_ROWS_PER_STEP, body, 0)


def _na_attn(z, dh, k_ctx, v_ctx, bias):
    b, t, _ = z.shape
    rows = t // GRID_W
    assert t % GRID_W == 0 and rows >= WIN_H and rows % NA_ROWS_PER_STEP == 0
    p = k_ctx.shape[1]
    nb = dh // LANES
    spec = lambda g: pl.BlockSpec((None, t, LANES), lambda bb, hp, g=g: (bb, 0, g * nb + hp))
    cspec = pl.BlockSpec((None, p, LANES), lambda bb, hp: (bb, 0, hp))
    return pl.pallas_call(
        functools.partial(_na_kernel, rows=rows),
        grid=(b, nb),
        in_specs=[spec(0), spec(1), spec(2), cspec, cspec,
                  pl.BlockSpec((None, WIN_H, 2 * GRID_W, WIN_H * GRID_W), lambda bb, hp: (hp, 0, 0, 0)),
                  spec(3)],
        out_specs=pl.BlockSpec((None, t, LANES), lambda bb, hp: (bb, 0, hp)),
        out_shape=jax.ShapeDtypeStruct((b, t, dh), BF16),
        scratch_shapes=[pltpu.VMEM((t, LANES), BF16), pltpu.VMEM((t, LANES), BF16)],
        compiler_params=_params("parallel", "parallel"),
        name="na_attention",
    )(z, z, z, k_ctx, v_ctx, bias, z)


N_OPS = 9


def _softplus(x):
    return jnp.maximum(x, 0.0) + jnp.log(1.0 + jnp.exp(-jnp.abs(x)))


def _rwkv_prep_kernel(r_ref, rp_ref, rn_ref, k_ref, kp_ref, kn_ref, v_ref, vp_ref, vn_ref,
                      cwr_ref, cwk_ref, cwv_ref, wdn_ref, adn_ref, wup_ref, aup_ref, w0_ref, a0_ref,
                      kk_ref, ka_ref, rk_ref, o_ref, bonus_ref):
    first = pl.program_id(1) == 0
    last = pl.program_id(1) == pl.num_programs(1) - 1
    ones = _head_ones()
    tw = jnp.tanh(wdn_ref[...]).astype(BF16)
    ad = adn_ref[...].astype(BF16)
    for s in range(bonus_ref.shape[1] // LANES):
        sl = slice(s * LANES, (s + 1) * LANES)
        r = _conv3(r_ref, rp_ref, rn_ref, cwr_ref[:, sl], first, last, sl)
        k = _conv3(k_ref, kp_ref, kn_ref, cwk_ref[:, sl], first, last, sl)
        v = _conv3(v_ref, vp_ref, vn_ref, cwv_ref[:, sl], first, last, sl)
        k_a = ka_ref[:, sl]
        kk = k * kk_ref[:, sl]
        nrm = jnp.sqrt(_head_sum(kk * kk, ones))
        kk = kk / jnp.maximum(nrm, 1e-12)
        o_ref[0, :, sl] = r
        o_ref[1, :, sl] = v
        o_ref[2, :, sl] = kk
        kd_sum = None
        for d in range(2):
            w_raw = -_softplus(-(w0_ref[d:d + 1, sl] + _dot(tw, wup_ref[d, :, sl].astype(BF16)))) - 0.5
            a = jax.nn.sigmoid(a0_ref[d:d + 1, sl] + _dot(ad, aup_ref[d, :, sl].astype(BF16)))
            kd = k * (1.0 + (a - 1.0) * k_a)
            o_ref[3 + d, :, sl] = -jnp.exp(w_raw)
            o_ref[5 + d, :, sl] = kd
            o_ref[7 + d, :, sl] = kk * a
            kd_sum = kd if kd_sum is None else kd_sum + kd
        bonus_ref[:, sl] = _head_sum(r * kd_sum * rk_ref[:, sl], ones) * v


def _rwkv_prep(z, dh, conv_w, w0, wup, a0, aup, k_k, k_a, r_k):
    b, l, _ = z.shape
    tl = _tile(l, 512, SUBLANES)
    tw = _tile(dh, 512, LANES)
    nb = dh // tw
    base = 4 * nb
    lora_blk = 8 * dh // LANES
    in_specs = (_conv_specs(tl, tw, l, base) + _conv_specs(tl, tw, l, base + nb)
                + _conv_specs(tl, tw, l, base + 2 * nb))
    in_specs += [pl.BlockSpec((3, tw), lambda bb, i, j, g=g: (0, g * nb + j)) for g in range(3)]
    in_specs += [pl.BlockSpec((None, tl, LANES), lambda bb, i, j: (bb, i, lora_blk)),
                 pl.BlockSpec((None, tl, LANES), lambda bb, i, j: (bb, i, lora_blk + 1)),
                 pl.BlockSpec((2, LANES, tw), lambda bb, i, j: (0, 0, j)),
                 pl.BlockSpec((2, LANES, tw), lambda bb, i, j: (0, 0, j)),
                 pl.BlockSpec((2, tw), lambda bb, i, j: (0, j)),
                 pl.BlockSpec((2, tw), lambda bb, i, j: (0, j)),
                 pl.BlockSpec((1, tw), lambda bb, i, j: (0, j)),
                 pl.BlockSpec((1, tw), lambda bb, i, j: (0, j)),
                 pl.BlockSpec((1, tw), lambda bb, i, j: (0, j))]
    zeros = jnp.zeros((LORA, dh), F32)
    wup_p = jnp.stack([jnp.concatenate([wup[0], zeros]), jnp.concatenate([zeros, wup[1]])])
    aup_p = jnp.stack([jnp.concatenate([aup[0], zeros]), jnp.concatenate([zeros, aup[1]])])
    return pl.pallas_call(
        _rwkv_prep_kernel,
        grid=(b, l // tl, nb),
        in_specs=in_specs,
        out_specs=[pl.BlockSpec((N_OPS, None, tl, tw), lambda bb, i, j: (0, bb, i, j)),
                   pl.BlockSpec((None, tl, tw), lambda bb, i, j: (bb, i, j))],
        out_shape=[jax.ShapeDtypeStruct((N_OPS, b, l, dh), F32), jax.ShapeDtypeStruct((b, l, dh), F32)],
        compiler_params=_params("parallel", "parallel", "parallel"),
        name="rwkv_prep",
    )(z, z, z, z, z, z, z, z, z, conv_w, conv_w, conv_w, z, z, wup_p, aup_p, w0, a0,
      k_k.reshape(1, dh), k_a.reshape(1, dh), r_k.reshape(1, dh))


SCAN_CHUNK = 64


def _split3(x):
    hi = x.astype(BF16)
    r1 = x - hi.astype(F32)
    mid = r1.astype(BF16)
    return hi, mid, (r1 - mid.astype(F32)).astype(BF16)


def _stack_heads(x, m0):
    return jnp.concatenate([jnp.where(m0, x, 0.0), jnp.where(m0, 0.0, x)], axis=0).astype(BF16)


def _scan_pair(lw, r, v, kk, kd, kb, s2, tri, strict2, incl2, m0, cm0, bd):
    c = lw.shape[0]
    hi, mid, lo = _split3(lw)
    a = _dot(tri, hi) + _dot(tri, mid) + _dot(tri, lo)
    yield
    atot = jnp.sum(lw, axis=0, keepdims=True)
    g_end = jnp.exp(atot - a)
    g_inv = jnp.exp(-a)
    kt = kk * jnp.exp(a - lw)
    rt = r * jnp.exp(a)
    lhs = jnp.concatenate([kt, rt], axis=0).astype(BF16)
    kbar = kd * g_inv
    bbar = kb * g_inv
    rhs = jnp.concatenate([_stack_heads(kbar, m0), _stack_heads(bbar, m0)], axis=0)
    g = _dot_nt(lhs, rhs)
    sh = _dot_nt(lhs, s2.astype(BF16))
    yield
    a_k = jnp.where(strict2, g[:c, :2 * c], 0.0)
    a_b = jnp.where(strict2, g[:c, 2 * c:], 0.0)
    a_rk = jnp.where(incl2, g[c:, :2 * c], 0.0)
    a_rb = jnp.where(incl2, g[c:, 2 * c:], 0.0)
    vs = _stack_heads(v, m0)
    x = sh[:c] + _dot(a_k.astype(BF16), vs)
    o_part = sh[c:] + _dot(a_rk.astype(BF16), vs)
    yield
    n = -a_b
    levels = int(math.log2(c))
    for lvl in range(levels):
        nb16 = n.astype(BF16)
        x = x + _dot(nb16, _stack_heads(x, m0))
        if lvl + 1 < levels:
            nbd = jnp.concatenate([jnp.where(cm0, n, 0.0), jnp.where(cm0, 0.0, n)], axis=0).astype(BF16)
            n = _dot(nb16, nbd)
        yield
    u = x
    o = o_part - _dot(a_rb.astype(BF16), _stack_heads(u, m0))
    w_val = jnp.concatenate([v, u], axis=0).astype(BF16)
    w_key = jnp.concatenate([kd * g_end, -(kb * g_end)], axis=0).astype(BF16)
    upd = lax.dot_general(w_val, w_key, (((0,), (0,)), ((), ())), preferred_element_type=F32)
    yield
    s_new = s2 * jnp.exp(atot) + jnp.where(bd, upd, 0.0)
    return o, s_new


def _scan_kernel(lw_ref, r_ref, v_ref, kk_ref, kd_ref, kb_ref, s0_ref, o_ref, sout_ref, s_ref, *, nchunk, npairs):
    d = pl.program_id(1)
    i = pl.program_id(2)

    @pl.when(i == 0)
    def _():
        s_ref[...] = s0_ref[...]

    c = SCAN_CHUNK
    sign = 1 - 2 * d
    tri = ((lax.broadcasted_iota(jnp.int32, (c, c), 1) - lax.broadcasted_iota(jnp.int32, (c, c), 0)) * sign
           <= 0).astype(BF16)
    diff2 = ((lax.broadcasted_iota(jnp.int32, (c, 2 * c), 1) & (c - 1))
             - lax.broadcasted_iota(jnp.int32, (c, 2 * c), 0)) * sign
    strict2 = diff2 < 0
    incl2 = diff2 <= 0
    m0 = _lane_mask((c, LANES))
    cm0 = lax.broadcasted_iota(jnp.int32, (c, 2 * c), 1) < c
    bd = (lax.broadcasted_iota(jnp.int32, (LANES, LANES), 0) // HEAD_DIM
          == lax.broadcasted_iota(jnp.int32, (LANES, LANES), 1) // HEAD_DIM)
    lanes = [slice(p * LANES, (p + 1) * LANES) for p in range(npairs)]
    results = _run_lockstep([
        _scan_pair(lw_ref[:, sl], r_ref[:, sl], v_ref[:, sl], kk_ref[:, sl], kd_ref[:, sl], kb_ref[:, sl],
                   s_ref[p], tri, strict2, incl2, m0, cm0, bd)
        for p, sl in enumerate(lanes)])
    for p, (o, s_new) in enumerate(results):
        o_ref[:, lanes[p]] = o
        s_ref[p] = s_new

    @pl.when(i == nchunk - 1)
    def _():
        sout_ref[...] = s_ref[...]


def _rwkv_scan(ops, s0):
    _, b, l, dh = ops.shape
    c = SCAN_CHUNK
    assert l % c == 0
    nchunk = l // c
    npairs = dh // LANES
    tb = lambda d, i: i + d * (nchunk - 1 - 2 * i)
    shared = lambda idx: pl.BlockSpec((None, None, c, dh), lambda bb, d, i: (idx, bb, tb(d, i), 0))
    perdir = lambda idx: pl.BlockSpec((None, None, c, dh), lambda bb, d, i: (idx + d, bb, tb(d, i), 0))
    sspec = pl.BlockSpec((None, None, npairs, LANES, LANES), lambda bb, d, i: (bb, d, 0, 0, 0))
    return pl.pallas_call(
        functools.partial(_scan_kernel, nchunk=nchunk, npairs=npairs),
        grid=(b, 2, nchunk),
        in_specs=[perdir(3), shared(0), shared(1), shared(2), perdir(5), perdir(7), sspec],
        out_specs=[pl.BlockSpec((None, None, c, dh), lambda bb, d, i: (d, bb, tb(d, i), 0)), sspec],
        out_shape=[jax.ShapeDtypeStruct((2, b, l, dh), F32),
                   jax.ShapeDtypeStruct((b, 2, npairs, LANES, LANES), F32)],
        scratch_shapes=[pltpu.VMEM((npairs, LANES, LANES), F32)],
        compiler_params=_params("parallel", "arbitrary", "arbitrary"),
        name="rwkv_scan",
    )(ops, ops, ops, ops, ops, ops, s0)


def _rwkv_post_kernel(of_ref, ob_ref, bonus_ref, gb_ref, g_ref, b_ref, o_ref):
    ones = _head_ones()
    for s in range(o_ref.shape[1] // LANES):
        sl = slice(s * LANES, (s + 1) * LANES)
        o = of_ref[:, sl] + ob_ref[:, sl]
        mu = _head_sum(o, ones) * (1.0 / HEAD_DIM)
        oc = o - mu
        var = _head_sum(oc * oc, ones) * (1.0 / HEAD_DIM)
        o = oc * lax.rsqrt(var + GN_EPS) * g_ref[:, sl] + b_ref[:, sl] + bonus_ref[:, sl]
        o_ref[:, sl] = (o * _silu(gb_ref[:, sl])).astype(BF16)


def _rwkv_post(o2, bonus, z, dh, gn_g, gn_b):
    _, b, l, _ = o2.shape
    tl = _tile(l, 512, SUBLANES)
    tw = _tile(dh, 512, LANES)
    nb = dh // tw
    gate_blk = 7 * nb
    return pl.pallas_call(
        _rwkv_post_kernel,
        grid=(b, l // tl, nb),
        in_specs=[pl.BlockSpec((None, None, tl, tw), lambda bb, i, j: (0, bb, i, j)),
                  pl.BlockSpec((None, None, tl, tw), lambda bb, i, j: (1, bb, i, j)),
                  pl.BlockSpec((None, tl, tw), lambda bb, i, j: (bb, i, j)),
                  pl.BlockSpec((None, tl, tw), lambda bb, i, j: (bb, i, gate_blk + j)),
                  pl.BlockSpec((1, tw), lambda bb, i, j: (0, j)),
                  pl.BlockSpec((1, tw), lambda bb, i, j: (0, j))],
        out_specs=pl.BlockSpec((None, tl, tw), lambda bb, i, j: (bb, i, j)),
        out_shape=jax.ShapeDtypeStruct((b, l, dh), BF16),
        compiler_params=_params("parallel", "parallel", "parallel"),
        name="rwkv_post",
    )(o2, o2, bonus, z, gn_g.reshape(1, dh), gn_b.reshape(1, dh))


def _rwkv_heads(z, dh, pe, s0):
    b, l, _ = z.shape
    nh = dh // HEAD_DIM
    npairs = dh // LANES
    ops, bonus = _rwkv_prep(z, dh, pe["conv_w"], pe["w0"], pe["wup"], pe["a0"], pe["aup"], pe["k_k"], pe["k_a"], pe["r_k"])
    if s0 is None:
        st = jnp.zeros((b, 2, npairs, LANES, LANES), F32)
    else:
        s = s0.astype(F32).reshape(b, 2, npairs, 2, HEAD_DIM, HEAD_DIM)
        zero = jnp.zeros_like(s[:, :, :, 0])
        st = jnp.concatenate([jnp.concatenate([s[:, :, :, 0], zero], axis=-1),
                              jnp.concatenate([zero, s[:, :, :, 1]], axis=-1)], axis=-2)
    o2, s_fin = _rwkv_scan(ops, st)
    s_fin = jnp.stack([s_fin[..., :HEAD_DIM, :HEAD_DIM], s_fin[..., HEAD_DIM:, HEAD_DIM:]], axis=3)
    s_fin = s_fin.reshape(b, 2, nh, HEAD_DIM, HEAD_DIM)
    mixed = _rwkv_post(o2, bonus, z, dh, pe["gn_g"], pe["gn_b"])
    return mixed, s_fin


def _even_layer(x, sc, sh, gt, ln_g, ln_b, alpha, pe, kv_ctx, s0):
    dh = pe["dh"]
    z = _modmm(x, sc, sh, pe["w_in"])
    if kv_ctx is None:
        mixed_a = _ctx_attn(z, dh)
    else:
        mixed_a = _na_attn(z, dh, kv_ctx[0], kv_ctx[1], pe["bias"])
    mixed_b, s_fin = _rwkv_heads(z, dh, pe, s0)
    out = _outln([mixed_a, mixed_b], [pe["w_out"][:dh], pe["w_out"][dh:]], x, gt, ln_g, ln_b, alpha)
    return out, z, s_fin


def _odd_layer(x, sc, sh, gt, ln_g, ln_b, alpha, po, spec, mats):
    c = po["c"]
    b, l, _ = x.shape
    p = _hy_planes(l)
    z = _modmm(x, sc, sh, po["w_in"], planes=p)
    x0, uu, uub = _hy_conv(z, po["conv_w"], po["conv_b"], c)
    mixed = _hy_fftconv(mats, spec, uub, uu, x0, z, po["skip"])
    mixed = mixed.transpose(0, 2, 1, 3).reshape(b, l, c)
    return _outln([mixed], [po["w_out"]], x, gt, ln_g, ln_b, alpha)


def kernel(x_prompt, x_sample, c, cache_a_k, cache_a_v, state_b, c_ctx, w_ada, b_ada, ln_g, ln_b,
           w_in_even, w_out_even, a_rpb, b_conv_w, b_w0, b_wup, b_a0, b_aup, b_kk, b_ka, b_rk,
           b_gn_g, b_gn_b, w_in_odd, w_out_odd, h_conv_w, h_conv_b, h_ffn_w1, h_ffn_b1, h_ffn_w2,
           h_ffn_b2, h_ffn_w3, h_ffn_b3, h_freq, h_ffn_wout, h_skip):
    depth, d, _ = w_ada.shape
    bx, lx, _ = x_prompt.shape
    by, ly, _ = x_sample.shape
    alpha = (2 * depth) ** 0.25
    dh = b_kk.shape[1]
    nh = dh // HEAD_DIM
    p_len = cache_a_k.shape[2]
    c_hy = h_skip.shape[1]

    rows = 1 + by
    rpad = -(-rows // SUBLANES) * SUBLANES
    cond = jnp.pad(jnp.concatenate([c_ctx[None, :], c], axis=0), ((0, rpad - rows), (0, 0)))
    mods = _ada_mod(cond, w_ada, b_ada)

    x = x_prompt
    y = x_sample
    mats_x = _hy_mats(lx) if depth > 1 else None
    mats_y = _hy_mats(ly) if depth > 1 else None
    new_k, new_v, new_s = [], [], []
    for l in range(depth):
        m = mods[l]
        sh_x, sc_x, gt_x = (jnp.broadcast_to(m[0:1, k * d:(k + 1) * d][None], (bx, 1, d)) for k in range(3))
        sh_y, sc_y, gt_y = (m[1:rows, k * d:(k + 1) * d][:, None, :] for k in range(3))
        if l % 2 == 0:
            e = l // 2
            pe = dict(dh=dh, w_in=w_in_even[e].astype(BF16), w_out=w_out_even[e].astype(BF16),
                      bias=_na_bias_table(a_rpb[e]), conv_w=b_conv_w[e], w0=b_w0[e], wup=b_wup[e], a0=b_a0[e],
                      aup=b_aup[e], k_k=b_kk[e], k_a=b_ka[e], r_k=b_rk[e], gn_g=b_gn_g[e], gn_b=b_gn_b[e])
            x, zx, s_fin = _even_layer(x, sc_x, sh_x, gt_x, ln_g[l], ln_b[l], alpha, pe, None, None)
            new_k.append(zx[..., dh:2 * dh].reshape(bx, lx, nh, HEAD_DIM))
            new_v.append(zx[..., 2 * dh:3 * dh].reshape(bx, lx, nh, HEAD_DIM))
            new_s.append(s_fin)
            kv = (cache_a_k[:, e].reshape(by, p_len, dh), cache_a_v[:, e].reshape(by, p_len, dh))
            y, _, _ = _even_layer(y, sc_y, sh_y, gt_y, ln_g[l], ln_b[l], alpha, pe, kv, state_b[:, e])
        else:
            o = l // 2
            po = dict(c=c_hy, w_in=w_in_odd[o].astype(BF16), w_out=w_out_odd[o].astype(BF16),
                      conv_w=h_conv_w[o], conv_b=h_conv_b[o], skip=h_skip[o])
            fargs = (h_ffn_w1[o], h_ffn_b1[o], h_ffn_w2[o], h_ffn_b2[o], h_ffn_w3[o], h_ffn_b3[o],
                     h_freq[o], h_ffn_wout[o])
            spec_x = _hy_spectrum(lx, c_hy, mats_x[0], *fargs)
            spec_y = _hy_spectrum(ly, c_hy, mats_y[0], *fargs)
            x = _odd_layer(x, sc_x, sh_x, gt_x, ln_g[l], ln_b[l], alpha, po, spec_x, mats_x)
            y = _odd_layer(y, sc_y, sh_y, gt_y, ln_g[l], ln_b[l], alpha, po, spec_y, mats_y)
    return (x, y, jnp.stack(new_k, axis=1), jnp.stack(new_v, axis=1), jnp.stack(new_s, axis=1))
```

```python
import functools
import math

import jax
import jax.numpy as jnp
from jax import lax
from jax.experimental import pallas as pl
from jax.experimental.pallas import tpu as pltpu

F32 = jnp.float32
BF16 = jnp.bfloat16

HEAD_DIM = 64
LORA = 64
GRID_W = 64
WIN_H = 8
WIN_W = 16
HY_TARGET = 1e-2
HY_FAST = 0.3
HY_SLOW = 1.5
LN_EPS = 1e-5
GN_EPS = 64e-5
NEG = -1e30

LANES = 128
SUBLANES = 8
VMEM_LIMIT = 56 * 1024 * 1024


def _tile(n, target, mult):
    if n <= target:
        return n
    t = (target // mult) * mult
    while t >= mult:
        if n % t == 0:
            return t
        t -= mult
    return n


def _params(*sem):
    return pltpu.CompilerParams(dimension_semantics=sem, vmem_limit_bytes=VMEM_LIMIT)


def _silu(x):
    return x * jax.nn.sigmoid(x)


def _dot(a, b):
    return jnp.dot(a, b, preferred_element_type=F32)


def _dot_nt(a, b):
    return lax.dot_general(a, b, (((1,), (1,)), ((), ())), preferred_element_type=F32)


def _run_lockstep(gens):
    results = [None] * len(gens)
    active = list(range(len(gens)))
    while active:
        for p in list(active):
            try:
                next(gens[p])
            except StopIteration as done:
                results[p] = done.value
                active.remove(p)
    return results


def _head_ones():
    r = lax.broadcasted_iota(jnp.int32, (LANES, LANES), 0) // HEAD_DIM
    c = lax.broadcasted_iota(jnp.int32, (LANES, LANES), 1) // HEAD_DIM
    return (r == c).astype(BF16)


def _head_sum(x, ones):
    hi = x.astype(BF16)
    r1 = x - hi.astype(F32)
    mid = r1.astype(BF16)
    lo = (r1 - mid.astype(F32)).astype(BF16)
    return _dot(hi, ones) + _dot(mid, ones) + _dot(lo, ones)


def _ada_kernel(c_ref, w_ref, b_ref, o_ref):
    c = c_ref[...]
    o_ref[...] = _dot(_silu(c).astype(BF16), w_ref[...].astype(BF16)) + b_ref[...]


def _ada_mod(cond, w_ada, b_ada):
    depth, d, n = w_ada.shape
    r = cond.shape[0]
    tn = _tile(n, 768, LANES)
    return pl.pallas_call(
        _ada_kernel,
        grid=(depth, n // tn),
        in_specs=[pl.BlockSpec((r, d), lambda l, j: (0, 0)),
                  pl.BlockSpec((None, d, tn), lambda l, j: (l, 0, j)),
                  pl.BlockSpec((None, 1, tn), lambda l, j: (l, 0, j))],
        out_specs=pl.BlockSpec((None, r, tn), lambda l, j: (l, 0, j)),
        out_shape=jax.ShapeDtypeStruct((depth, r, n), F32),
        compiler_params=_params("parallel", "parallel"),
        name="ada_mod",
    )(cond, w_ada, b_ada.reshape(depth, 1, n))


def _modmm_kernel(*refs, nx, planes):
    x_refs = refs[:nx]
    sc_ref, sh_ref, w_ref, o_ref, h_ref = refs[nx:]
    p = planes or 1
    rows = x_refs[0].shape[0] // p
    cols = x_refs[0].shape[1]

    @pl.when(pl.program_id(2) == 0)
    def _():
        for c, x_ref in enumerate(x_refs):
            cs = slice(c * cols, (c + 1) * cols)
            scale = 1.0 + sc_ref[:, cs]
            shift = sh_ref[:, cs]
            for t2 in range(p):
                x = x_ref[...] if p == 1 else x_ref[pl.ds(t2, rows, stride=p), :]
                h_ref[t2 * rows:(t2 + 1) * rows, cs] = (x * scale + shift).astype(BF16)

    out = _dot(h_ref[...], w_ref[...])
    if planes is None:
        o_ref[...] = out
    else:
        for t2 in range(p):
            o_ref[t2] = out[t2 * rows:(t2 + 1) * rows]


def _modmm(x, sc, sh, w, planes=None):
    b, l, d = x.shape
    n = w.shape[1]
    tn = _tile(n, 1408, LANES)
    tm = _tile(l, 1024, SUBLANES)
    if planes is None:
        out_spec = pl.BlockSpec((None, tm, tn), lambda bb, i, j: (bb, i, j))
        out_shape = (b, l, n)
    else:
        assert tm % (planes * 16) == 0
        out_spec = pl.BlockSpec((None, planes, tm // planes, tn), lambda bb, i, j: (bb, 0, i, j))
        out_shape = (b, planes, l // planes, n)
    xw = LANES if (planes or 1) > 1 else d
    nx = d // xw
    return pl.pallas_call(
        functools.partial(_modmm_kernel, nx=nx, planes=planes),
        grid=(b, l // tm, n // tn),
        in_specs=[pl.BlockSpec((None, tm, xw), lambda bb, i, j, c=c: (bb, i, c)) for c in range(nx)]
        + [pl.BlockSpec((None, 1, d), lambda bb, i, j: (bb, 0, 0)),
           pl.BlockSpec((None, 1, d), lambda bb, i, j: (bb, 0, 0)),
           pl.BlockSpec((d, tn), lambda bb, i, j: (0, j))],
        out_specs=out_spec,
        out_shape=jax.ShapeDtypeStruct(out_shape, F32),
        scratch_shapes=[pltpu.VMEM((tm, d), BF16)],
        compiler_params=_params("parallel", "parallel", "arbitrary"),
        name="mod_in_proj",
    )(*([x] * nx), sc, sh, w)


def _outln_kernel(*refs, nparts, alpha):
    m_refs = refs[:nparts]
    w_refs = refs[nparts:2 * nparts]
    x_ref, gt_ref, g_ref, b_ref, o_ref = refs[2 * nparts:]
    acc = _dot(m_refs[0][...], w_refs[0][...])
    for p in range(1, nparts):
        acc = acc + _dot(m_refs[p][...], w_refs[p][...])
    y = alpha * x_ref[...] + gt_ref[...] * acc
    mu = jnp.mean(y, axis=-1, keepdims=True)
    yc = y - mu
    var = jnp.mean(yc * yc, axis=-1, keepdims=True)
    o_ref[...] = yc * lax.rsqrt(var + LN_EPS) * g_ref[...] + b_ref[...]


def _outln(parts, ws, x, gt, ln_g, ln_b, alpha):
    b, l, d = x.shape
    nparts = len(parts)
    tl = _tile(l, 512, SUBLANES)
    row_spec = pl.BlockSpec((None, tl, d), lambda bb, i: (bb, i, 0))
    in_specs = [pl.BlockSpec((None, tl, p.shape[2]), lambda bb, i: (bb, i, 0)) for p in parts]
    in_specs += [pl.BlockSpec(w.shape, lambda bb, i: (0, 0)) for w in ws]
    in_specs += [row_spec,
                 pl.BlockSpec((None, 1, d), lambda bb, i: (bb, 0, 0)),
                 pl.BlockSpec((1, d), lambda bb, i: (0, 0)),
                 pl.BlockSpec((1, d), lambda bb, i: (0, 0))]
    return pl.pallas_call(
        functools.partial(_outln_kernel, nparts=nparts, alpha=alpha),
        grid=(b, l // tl),
        in_specs=in_specs,
        out_specs=row_spec,
        out_shape=jax.ShapeDtypeStruct((b, l, d), F32),
        compiler_params=_params("parallel", "parallel"),
        name="out_proj_ln",
    )(*parts, *ws, x, gt, ln_g.reshape(1, d), ln_b.reshape(1, d))


def _conv_specs(tl, tc, l, colblk):
    g = tl // SUBLANES
    last = l // SUBLANES - 1
    return [
        pl.BlockSpec((None, tl, tc), lambda bb, i, j: (bb, i, colblk + j)),
        pl.BlockSpec((None, SUBLANES, tc), lambda bb, i, j: (bb, jnp.maximum(i * g - 1, 0), colblk + j)),
        pl.BlockSpec((None, SUBLANES, tc), lambda bb, i, j: (bb, jnp.minimum((i + 1) * g, last), colblk + j)),
    ]


def _conv3(x_ref, p_ref, n_ref, w, first, last, lanes=slice(None)):
    x = x_ref[:, lanes]
    tl = x.shape[0]
    row = lax.broadcasted_iota(jnp.int32, x.shape, 0)
    prow = jnp.where(first, 0.0, p_ref[SUBLANES - 1:SUBLANES, lanes])
    nrow = jnp.where(last, 0.0, n_ref[0:1, lanes])
    xm = jnp.where(row == 0, prow, pltpu.roll(x, 1, 0))
    xq = jnp.where(row == tl - 1, nrow, pltpu.roll(x, tl - 1, 0))
    return xm * w[0:1, :] + x * w[1:2, :] + xq * w[2:3, :]


HY_PLANES = 16
HY_PLANE_ROWS = 256


def _hy_planes(l):
    p = HY_PLANES
    while p > 1 and l // p < HY_PLANE_ROWS:
        p //= 2
    return p


def _hy_cols(p):
    return LANES * (HY_PLANES // p)


def _plane_conv3(z_ref, w, bias):
    p, rows, _ = z_ref.shape
    row = lax.broadcasted_iota(jnp.int32, z_ref.shape[1:], 0)
    before = jnp.where(row == 0, 0.0, pltpu.roll(z_ref[p - 1], 1, 0))
    after = jnp.where(row == rows - 1, 0.0, pltpu.roll(z_ref[0], rows - 1, 0))
    out = []
    for t2 in range(p):
        prev = z_ref[t2 - 1] if t2 > 0 else before
        nxt = z_ref[t2 + 1] if t2 < p - 1 else after
        out.append(prev * w[0:1, :] + z_ref[t2] * w[1:2, :] + nxt * w[2:3, :] + bias)
    return out


def _hy_conv_kernel(z0_ref, z1_ref, z2_ref, w0_ref, w1_ref, w2_ref, b0_ref, b1_ref, b2_ref,
                    x0o_ref, uu_ref, uub_ref):
    x0 = _plane_conv3(z0_ref, w0_ref[...], b0_ref[...])
    x1 = _plane_conv3(z1_ref, w1_ref[...], b1_ref[...])
    v = _plane_conv3(z2_ref, w2_ref[...], b2_ref[...])
    for t2 in range(len(x0)):
        uu = x1[t2] * v[t2]
        x0o_ref[t2] = x0[t2]
        uu_ref[t2] = uu
        uub_ref[t2] = uu.astype(BF16)


def _hy_conv(z, conv_w, conv_b, c):
    b, p, rows, _ = z.shape
    tc = _tile(c, 2 * _hy_cols(p), LANES)
    nc = c // tc
    in_specs = [pl.BlockSpec((None, p, rows, tc), lambda bb, j, k=k: (bb, 0, 0, k * nc + j)) for k in range(3)]
    in_specs += [pl.BlockSpec((3, tc), lambda bb, j, k=k: (0, k * nc + j)) for k in range(3)]
    in_specs += [pl.BlockSpec((1, tc), lambda bb, j, k=k: (0, k * nc + j)) for k in range(3)]
    ospec = pl.BlockSpec((None, p, rows, tc), lambda bb, j: (bb, 0, 0, j))
    cb = conv_b.reshape(1, 3 * c)
    return pl.pallas_call(
        _hy_conv_kernel,
        grid=(b, nc),
        in_specs=in_specs,
        out_specs=[ospec, ospec, ospec],
        out_shape=[jax.ShapeDtypeStruct((b, p, rows, c), F32), jax.ShapeDtypeStruct((b, p, rows, c), F32),
                   jax.ShapeDtypeStruct((b, p, rows, c), BF16)],
        compiler_params=_params("parallel", "parallel"),
        name="hyena_gate_conv",
    )(z, z, z, conv_w, conv_w, conv_w, cb, cb, cb)


def _hy_filter_kernel(feats_ref, t_ref, fw1_ref, fb1_ref, fw2_ref, fb2_ref, fw3_ref, fb3_ref, freq_ref,
                      wf_ref, wb_ref, dl_ref, o_ref):
    freq = freq_ref[...]
    hdn = jnp.sin(freq[0:1, :] * (_dot(feats_ref[...].astype(BF16), fw1_ref[...].astype(BF16)) + fb1_ref[...]))
    hdn = jnp.sin(freq[1:2, :] * (_dot(hdn.astype(BF16), fw2_ref[...].astype(BF16)) + fb2_ref[...]))
    hdn = jnp.sin(freq[2:3, :] * (_dot(hdn.astype(BF16), fw3_ref[...].astype(BF16)) + fb3_ref[...]))
    hb16 = hdn.astype(BF16)
    window = jnp.exp(-t_ref[...] * dl_ref[...])
    h_f = _dot(hb16, wf_ref[...].astype(BF16)) * window
    h_b = _dot(hb16, wb_ref[...].astype(BF16)) * window
    h_b = h_b * (t_ref[...] > 0.0).astype(F32)
    norm = jnp.sum(jnp.abs(h_f), axis=0, keepdims=True) + jnp.sum(jnp.abs(h_b), axis=0, keepdims=True)
    inv = 1.0 / norm
    o_ref[0] = (h_f * inv).astype(BF16)
    o_ref[1] = (h_b * inv).astype(BF16)


def _hy_filter(l, c, fw1, fb1, fw2, fb2, fw3, fb3, freq, fwout):
    emb, order = fw1.shape
    bands_n = (emb - 1) // 2
    t = jnp.linspace(0.0, 1.0, l, dtype=F32)[:, None]
    ang = 2.0 * math.pi * jnp.arange(l, dtype=F32)[:, None] / l
    bands = jnp.linspace(1e-4, bands_n - 1, bands_n, dtype=F32)
    feats = jnp.concatenate([t, jnp.cos(ang * bands), -jnp.sin(ang * bands)], axis=-1)
    feats = jnp.pad(feats, ((0, 0), (0, LANES - emb)))
    p = _hy_planes(l)
    plane_major = lambda a: a.reshape(l // p, p, a.shape[1]).transpose(1, 0, 2).reshape(l, a.shape[1])
    feats, t = plane_major(feats), plane_major(t)
    fw1p = jnp.pad(fw1, ((0, LANES - emb), (0, 0)))
    deltas = jnp.abs(jnp.linspace(math.log(HY_TARGET) / HY_SLOW, math.log(HY_TARGET) / HY_FAST, c, dtype=F32))
    tc = _tile(c, 256, LANES)
    nc = c // tc
    full = lambda shape: pl.BlockSpec(shape, lambda j: tuple(0 for _ in shape))
    return pl.pallas_call(
        _hy_filter_kernel,
        grid=(nc,),
        in_specs=[full((l, LANES)), full((l, 1)), full((LANES, order)), full((1, order)),
                  full((order, order)), full((1, order)), full((order, order)), full((1, order)),
                  full((3, order)),
                  pl.BlockSpec((order, tc), lambda j: (0, j)),
                  pl.BlockSpec((order, tc), lambda j: (0, nc + j)),
                  pl.BlockSpec((1, tc), lambda j: (0, j))],
        out_specs=pl.BlockSpec((2, l, tc), lambda j: (0, 0, j)),
        out_shape=jax.ShapeDtypeStruct((2, l, c), BF16),
        compiler_params=_params("parallel"),
        name="hyena_filter",
    )(feats, t, fw1p, fb1.reshape(1, order), fw2, fb2.reshape(1, order), fw3, fb3.reshape(1, order),
      freq, fwout, fwout, deltas.reshape(1, c))


def _hy_mats(l):
    p = _hy_planes(l)
    rows = l // p
    n = 2 * l
    kq = jnp.arange(rows, dtype=jnp.int32)
    t = p * jnp.arange(rows, dtype=jnp.int32)[None, None, :] + jnp.arange(p, dtype=jnp.int32)[:, None, None]
    phase = ((2 * kq[None, :, None] + 1) * t) % (2 * n)
    ang = phase.astype(F32) * (math.pi / n)
    cos, sin = jnp.cos(ang), jnp.sin(ang)
    fwd = jnp.concatenate([cos, -sin], axis=1)
    inv = (2.0 / n) * jnp.concatenate([cos.transpose(0, 2, 1), -sin.transpose(0, 2, 1)], axis=2)
    return fwd.astype(BF16), inv.astype(BF16)


def _plane_butterflies(s_ref, sign, dif):
    p = s_ref.shape[1]
    sizes = [p >> i for i in range(int(math.log2(p)))]
    if not dif:
        sizes = sizes[::-1]
    for size in sizes:
        half = size // 2
        for start in range(0, p, size):
            for k in range(half):
                ang = sign * 2.0 * math.pi * k / size
                wr, wi = math.cos(ang), math.sin(ang)
                i0, i1 = start + k, start + k + half
                ur, ui, vr, vi = s_ref[0, i0], s_ref[1, i0], s_ref[0, i1], s_ref[1, i1]

                def twiddle(xr, xi):
                    if k == 0:
                        return xr, xi
                    if 4 * k == size:
                        return (-xi, xr) if sign > 0 else (xi, -xr)
                    return xr * wr - xi * wi, xr * wi + xi * wr

                if dif:
                    s_ref[0, i0] = ur + vr
                    s_ref[1, i0] = ui + vi
                    tr, ti = twiddle(ur - vr, ui - vi)
                    s_ref[0, i1] = tr
                    s_ref[1, i1] = ti
                else:
                    tr, ti = twiddle(vr, vi)
                    s_ref[0, i0] = ur + tr
                    s_ref[1, i0] = ui + ti
                    s_ref[0, i1] = ur - tr
                    s_ref[1, i1] = ui - ti


def _plane_spectrum(x_ref, f_ref, s_ref):
    p, rows, _ = x_ref.shape
    for t2 in range(p):
        y = _dot(f_ref[t2], x_ref[t2])
        s_ref[0, t2] = y[:rows]
        s_ref[1, t2] = y[rows:]
    _plane_butterflies(s_ref, -1, dif=True)


def _hy_spec_kernel(h_ref, f_ref, o_ref, s_ref):
    p = s_ref.shape[1]
    _plane_spectrum(h_ref.at[0], f_ref, s_ref)
    for i in range(p):
        o_ref[0, i] = s_ref[0, i]
        o_ref[1, i] = s_ref[1, i]
    _plane_spectrum(h_ref.at[1], f_ref, s_ref)
    for i in range(p):
        o_ref[0, i] = o_ref[0, i] + s_ref[0, i]
        o_ref[1, i] = o_ref[1, i] - s_ref[1, i]


def _hy_spectrum(l, c, fwd, fw1, fb1, fw2, fb2, fw3, fb3, freq, fwout):
    p = _hy_planes(l)
    rows = l // p
    hfb = _hy_filter(l, c, fw1, fb1, fw2, fb2, fw3, fb3, freq, fwout).reshape(2, p, rows, c)
    tn = _tile(c, _hy_cols(p), LANES)
    return pl.pallas_call(
        _hy_spec_kernel,
        grid=(c // tn,),
        in_specs=[pl.BlockSpec((2, p, rows, tn), lambda j: (0, 0, 0, j)),
                  pl.BlockSpec((p, 2 * rows, rows), lambda j: (0, 0, 0))],
        out_specs=pl.BlockSpec((2, p, rows, tn), lambda j: (0, 0, 0, j)),
        out_shape=jax.ShapeDtypeStruct((2, p, rows, c), F32),
        scratch_shapes=[pltpu.VMEM((2, p, rows, tn), F32)],
        compiler_params=_params("parallel"),
        name="hyena_filter_spectrum",
    )(hfb, fwd)


def _hy_fftconv_kernel(x_ref, f_ref, g_ref, h_ref, uu_ref, x0_ref, gz_ref, skip_ref, o_ref, s_ref):
    p, rows, _ = x_ref.shape
    _plane_spectrum(x_ref, f_ref, s_ref)
    for i in range(p):
        re, im, hre, him = s_ref[0, i], s_ref[1, i], h_ref[0, i], h_ref[1, i]
        s_ref[0, i] = re * hre - im * him
        s_ref[1, i] = re * him + im * hre
    _plane_butterflies(s_ref, 1, dif=False)
    skip = skip_ref[...]
    for t2 in range(p):
        spec = jnp.concatenate([s_ref[0, t2], s_ref[1, t2]], axis=0).astype(BF16)
        y = _dot(g_ref[t2], spec) + uu_ref[t2] * skip
        o_ref[t2] = (_silu(gz_ref[t2]) * (x0_ref[t2] * y)).astype(BF16)


def _hy_fftconv(mats, spec, uub, uu, x0, z, skip):
    b, p, rows, c = uu.shape
    tn = _tile(c, _hy_cols(p), LANES)
    nc = c // tn
    tile = pl.BlockSpec((None, p, rows, tn), lambda bb, j: (bb, 0, 0, j))
    return pl.pallas_call(
        _hy_fftconv_kernel,
        grid=(b, nc),
        in_specs=[tile,
                  pl.BlockSpec((p, 2 * rows, rows), lambda bb, j: (0, 0, 0)),
                  pl.BlockSpec((p, rows, 2 * rows), lambda bb, j: (0, 0, 0)),
                  pl.BlockSpec((2, p, rows, tn), lambda bb, j: (0, 0, 0, j)),
                  tile, tile,
                  pl.BlockSpec((None, p, rows, tn), lambda bb, j: (bb, 0, 0, 3 * nc + j)),
                  pl.BlockSpec((1, tn), lambda bb, j: (0, j))],
        out_specs=tile,
        out_shape=jax.ShapeDtypeStruct((b, p, rows, c), BF16),
        scratch_shapes=[pltpu.VMEM((2, p, rows, tn), F32)],
        compiler_params=_params("parallel", "parallel"),
        name="hyena_fftconv",
    )(uub, mats[0], mats[1], spec, uu, x0, z, skip.reshape(1, c))


def _lane_mask(shape):
    return lax.broadcasted_iota(jnp.int32, shape, 1) < HEAD_DIM


def _ctx_attn_kernel(q_ref, k_ref, v_ref, ga_ref, o_ref):
    q = q_ref[...]
    k = k_ref[...].astype(BF16)
    v = v_ref[...].astype(BF16)
    m0 = _lane_mask(q.shape)
    scale = HEAD_DIM ** -0.5
    outs = []
    for h in range(2):
        qh = jnp.where(m0 if h == 0 else jnp.logical_not(m0), q, 0.0).astype(BF16)
        s = _dot_nt(qh, k) * scale
        p = jnp.exp(s - jnp.max(s, axis=-1, keepdims=True))
        den = jnp.sum(p, axis=-1, keepdims=True)
        outs.append(_dot(p.astype(BF16), v) / den)
    o = jnp.where(m0, outs[0], outs[1])
    o_ref[...] = (o * _silu(ga_ref[...])).astype(BF16)


def _ctx_attn(z, dh):
    b, l, _ = z.shape
    nb = dh // LANES
    spec = lambda g: pl.BlockSpec((None, l, LANES), lambda bb, hp, g=g: (bb, 0, g * nb + hp))
    return pl.pallas_call(
        _ctx_attn_kernel,
        grid=(b, nb),
        in_specs=[spec(0), spec(1), spec(2), spec(3)],
        out_specs=pl.BlockSpec((None, l, LANES), lambda bb, hp: (bb, 0, hp)),
        out_shape=jax.ShapeDtypeStruct((b, l, dh), BF16),
        compiler_params=_params("parallel", "parallel"),
        name="ctx_attention",
    )(z, z, z, z)


def _na_bias_table(rpb):
    c = jnp.arange(GRID_W)[:, None]
    kc = jnp.arange(GRID_W)[None, :]
    qstart = jnp.clip(c - WIN_W // 2, 0, GRID_W - WIN_W)
    mask = (kc >= qstart) & (kc < qstart + WIN_W)
    cidx = jnp.clip(kc - c, 1 - WIN_W, WIN_W - 1) + WIN_W - 1
    dr = jnp.arange(WIN_H)[:, None] + jnp.arange(WIN_H)[None, :]
    t = rpb[:, dr][..., cidx]
    t = jnp.where(mask, t, NEG)
    nh = rpb.shape[0]
    t = t.transpose(1, 0, 3, 2, 4).reshape(WIN_H, nh // 2, 2 * GRID_W, WIN_H * GRID_W)
    return t.transpose(1, 0, 2, 3)


NA_ROWS_PER_STEP = 4


def _na_kernel(q_ref, k_ref, v_ref, kc_ref, vc_ref, bias_ref, ga_ref, o_ref, kb_ref, vb_ref, *, rows):
    kb_ref[...] = k_ref[...].astype(BF16)
    vb_ref[...] = v_ref[...].astype(BF16)
    kc = kc_ref[...].astype(BF16)
    vc = vc_ref[...].astype(BF16)
    m0 = _lane_mask((GRID_W, LANES))
    scale = HEAD_DIM ** -0.5
    win = WIN_H * GRID_W

    def one_row(r):
        rs = jnp.clip(r - WIN_H // 2, 0, rows - WIN_H)
        d0 = rs - r + WIN_H - 1
        q0 = pl.multiple_of(r * GRID_W, GRID_W)
        k0 = pl.multiple_of(rs * GRID_W, GRID_W)
        q = q_ref[pl.ds(q0, GRID_W), :]
        q2 = jnp.concatenate([jnp.where(m0, q, 0.0), jnp.where(m0, 0.0, q)], axis=0).astype(BF16)
        kw = kb_ref[pl.ds(k0, win), :]
        vw = vb_ref[pl.ds(k0, win), :]
        s_nb = _dot_nt(q2, kw)
        s_cx = _dot_nt(q2, kc)
        yield
        s_nb = s_nb * scale + bias_ref[d0]
        s_cx = s_cx * scale
        m = jnp.maximum(jnp.max(s_nb, axis=-1, keepdims=True), jnp.max(s_cx, axis=-1, keepdims=True))
        p_nb = jnp.exp(s_nb - m)
        p_cx = jnp.exp(s_cx - m)
        den = jnp.sum(p_nb, axis=-1, keepdims=True) + jnp.sum(p_cx, axis=-1, keepdims=True)
        o2 = _dot(p_nb.astype(BF16), vw) + _dot(p_cx.astype(BF16), vc)
        yield
        o2 = o2 / den
        o = jnp.where(m0, o2[:GRID_W], o2[GRID_W:])
        ga = ga_ref[pl.ds(q0, GRID_W), :]
        o_ref[pl.ds(q0, GRID_W), :] = (o * _silu(ga)).astype(BF16)

    def body(i, carry):
        _run_lockstep([one_row(i * NA_ROWS_PER_STEP + u) for u in range(NA_ROWS_PER_STEP)])
        return carry

    lax.fori_loop(0, rows // NA_ROWS_PER_STEP, body, 0)


def _na_attn(z, dh, k_ctx, v_ctx, bias):
    b, t, _ = z.shape
    rows = t // GRID_W
    assert t % GRID_W == 0 and rows >= WIN_H and rows % NA_ROWS_PER_STEP == 0
    p = k_ctx.shape[1]
    nb = dh // LANES
    spec = lambda g: pl.BlockSpec((None, t, LANES), lambda bb, hp, g=g: (bb, 0, g * nb + hp))
    cspec = pl.BlockSpec((None, p, LANES), lambda bb, hp: (bb, 0, hp))
    return pl.pallas_call(
        functools.partial(_na_kernel, rows=rows),
        grid=(b, nb),
        in_specs=[spec(0), spec(1), spec(2), cspec, cspec,
                  pl.BlockSpec((None, WIN_H, 2 * GRID_W, WIN_H * GRID_W), lambda bb, hp: (hp, 0, 0, 0)),
                  spec(3)],
        out_specs=pl.BlockSpec((None, t, LANES), lambda bb, hp: (bb, 0, hp)),
        out_shape=jax.ShapeDtypeStruct((b, t, dh), BF16),
        scratch_shapes=[pltpu.VMEM((t, LANES), BF16), pltpu.VMEM((t, LANES), BF16)],
        compiler_params=_params("parallel", "parallel"),
        name="na_attention",
    )(z, z, z, k_ctx, v_ctx, bias, z)


N_OPS = 9


def _softplus(x):
    return jnp.maximum(x, 0.0) + jnp.log(1.0 + jnp.exp(-jnp.abs(x)))


def _rwkv_prep_kernel(r_ref, rp_ref, rn_ref, k_ref, kp_ref, kn_ref, v_ref, vp_ref, vn_ref,
                      cwr_ref, cwk_ref, cwv_ref, wdn_ref, adn_ref, wup_ref, aup_ref, w0_ref, a0_ref,
                      kk_ref, ka_ref, rk_ref, o_ref, bonus_ref):
    first = pl.program_id(1) == 0
    last = pl.program_id(1) == pl.num_programs(1) - 1
    ones = _head_ones()
    tw = jnp.tanh(wdn_ref[...]).astype(BF16)
    ad = adn_ref[...].astype(BF16)
    for s in range(bonus_ref.shape[1] // LANES):
        sl = slice(s * LANES, (s + 1) * LANES)
        r = _conv3(r_ref, rp_ref, rn_ref, cwr_ref[:, sl], first, last, sl)
        k = _conv3(k_ref, kp_ref, kn_ref, cwk_ref[:, sl], first, last, sl)
        v = _conv3(v_ref, vp_ref, vn_ref, cwv_ref[:, sl], first, last, sl)
        k_a = ka_ref[:, sl]
        kk = k * kk_ref[:, sl]
        nrm = jnp.sqrt(_head_sum(kk * kk, ones))
        kk = kk / jnp.maximum(nrm, 1e-12)
        o_ref[0, :, sl] = r
        o_ref[1, :, sl] = v
        o_ref[2, :, sl] = kk
        kd_sum = None
        for d in range(2):
            w_raw = -_softplus(-(w0_ref[d:d + 1, sl] + _dot(tw, wup_ref[d, :, sl].astype(BF16)))) - 0.5
            a = jax.nn.sigmoid(a0_ref[d:d + 1, sl] + _dot(ad, aup_ref[d, :, sl].astype(BF16)))
            kd = k * (1.0 + (a - 1.0) * k_a)
            o_ref[3 + d, :, sl] = -jnp.exp(w_raw)
            o_ref[5 + d, :, sl] = kd
            o_ref[7 + d, :, sl] = kk * a
            kd_sum = kd if kd_sum is None else kd_sum + kd
        bonus_ref[:, sl] = _head_sum(r * kd_sum * rk_ref[:, sl], ones) * v


def _rwkv_prep(z, dh, conv_w, w0, wup, a0, aup, k_k, k_a, r_k):
    b, l, _ = z.shape
    tl = _tile(l, 512, SUBLANES)
    tw = _tile(dh, 512, LANES)
    nb = dh // tw
    base = 4 * nb
    lora_blk = 8 * dh // LANES
    in_specs = (_conv_specs(tl, tw, l, base) + _conv_specs(tl, tw, l, base + nb)
                + _conv_specs(tl, tw, l, base + 2 * nb))
    in_specs += [pl.BlockSpec((3, tw), lambda bb, i, j, g=g: (0, g * nb + j)) for g in range(3)]
    in_specs += [pl.BlockSpec((None, tl, LANES), lambda bb, i, j: (bb, i, lora_blk)),
                 pl.BlockSpec((None, tl, LANES), lambda bb, i, j: (bb, i, lora_blk + 1)),
                 pl.BlockSpec((2, LANES, tw), lambda bb, i, j: (0, 0, j)),
                 pl.BlockSpec((2, LANES, tw), lambda bb, i, j: (0, 0, j)),
                 pl.BlockSpec((2, tw), lambda bb, i, j: (0, j)),
                 pl.BlockSpec((2, tw), lambda bb, i, j: (0, j)),
                 pl.BlockSpec((1, tw), lambda bb, i, j: (0, j)),
                 pl.BlockSpec((1, tw), lambda bb, i, j: (0, j)),
                 pl.BlockSpec((1, tw), lambda bb, i, j: (0, j))]
    zeros = jnp.zeros((LORA, dh), F32)
    wup_p = jnp.stack([jnp.concatenate([wup[0], zeros]), jnp.concatenate([zeros, wup[1]])])
    aup_p = jnp.stack([jnp.concatenate([aup[0], zeros]), jnp.concatenate([zeros, aup[1]])])
    return pl.pallas_call(
        _rwkv_prep_kernel,
        grid=(b, l // tl, nb),
        in_specs=in_specs,
        out_specs=[pl.BlockSpec((N_OPS, None, tl, tw), lambda bb, i, j: (0, bb, i, j)),
                   pl.BlockSpec((None, tl, tw), lambda bb, i, j: (bb, i, j))],
        out_shape=[jax.ShapeDtypeStruct((N_OPS, b, l, dh), F32), jax.ShapeDtypeStruct((b, l, dh), F32)],
        compiler_params=_params("parallel", "parallel", "parallel"),
        name="rwkv_prep",
    )(z, z, z, z, z, z, z, z, z, conv_w, conv_w, conv_w, z, z, wup_p, aup_p, w0, a0,
      k_k.reshape(1, dh), k_a.reshape(1, dh), r_k.reshape(1, dh))


SCAN_CHUNK = 64


def _split3(x):
    hi = x.astype(BF16)
    r1 = x - hi.astype(F32)
    mid = r1.astype(BF16)
    return hi, mid, (r1 - mid.astype(F32)).astype(BF16)


def _stack_heads(x, m0):
    return jnp.concatenate([jnp.where(m0, x, 0.0), jnp.where(m0, 0.0, x)], axis=0).astype(BF16)


def _scan_pair(lw, r, v, kk, kd, kb, s2, tri, strict2, incl2, m0, cm0, bd):
    c = lw.shape[0]
    hi, mid, lo = _split3(lw)
    a = _dot(tri, hi) + _dot(tri, mid) + _dot(tri, lo)
    yield
    atot = jnp.sum(lw, axis=0, keepdims=True)
    g_end = jnp.exp(atot - a)
    g_inv = jnp.exp(-a)
    kt = kk * jnp.exp(a - lw)
    rt = r * jnp.exp(a)
    lhs = jnp.concatenate([kt, rt], axis=0).astype(BF16)
    kbar = kd * g_inv
    bbar = kb * g_inv
    rhs = jnp.concatenate([_stack_heads(kbar, m0), _stack_heads(bbar, m0)], axis=0)
    g = _dot_nt(lhs, rhs)
    sh = _dot_nt(lhs, s2.astype(BF16))
    yield
    a_k = jnp.where(strict2, g[:c, :2 * c], 0.0)
    a_b = jnp.where(strict2, g[:c, 2 * c:], 0.0)
    a_rk = jnp.where(incl2, g[c:, :2 * c], 0.0)
    a_rb = jnp.where(incl2, g[c:, 2 * c:], 0.0)
    vs = _stack_heads(v, m0)
    x = sh[:c] + _dot(a_k.astype(BF16), vs)
    o_part = sh[c:] + _dot(a_rk.astype(BF16), vs)
    yield
    n = -a_b
    levels = int(math.log2(c))
    for lvl in range(levels):
        nb16 = n.astype(BF16)
        x = x + _dot(nb16, _stack_heads(x, m0))
        if lvl + 1 < levels:
            nbd = jnp.concatenate([jnp.where(cm0, n, 0.0), jnp.where(cm0, 0.0, n)], axis=0).astype(BF16)
            n = _dot(nb16, nbd)
        yield
    u = x
    o = o_part - _dot(a_rb.astype(BF16), _stack_heads(u, m0))
    w_val = jnp.concatenate([v, u], axis=0).astype(BF16)
    w_key = jnp.concatenate([kd * g_end, -(kb * g_end)], axis=0).astype(BF16)
    upd = lax.dot_general(w_val, w_key, (((0,), (0,)), ((), ())), preferred_element_type=F32)
    yield
    s_new = s2 * jnp.exp(atot) + jnp.where(bd, upd, 0.0)
    return o, s_new


def _scan_kernel(lw_ref, r_ref, v_ref, kk_ref, kd_ref, kb_ref, s0_ref, o_ref, sout_ref, s_ref, *, nchunk, npairs):
    d = pl.program_id(1)
    i = pl.program_id(2)

    @pl.when(i == 0)
    def _():
        s_ref[...] = s0_ref[...]

    c = SCAN_CHUNK
    sign = 1 - 2 * d
    tri = ((lax.broadcasted_iota(jnp.int32, (c, c), 1) - lax.broadcasted_iota(jnp.int32, (c, c), 0)) * sign
           <= 0).astype(BF16)
    diff2 = ((lax.broadcasted_iota(jnp.int32, (c, 2 * c), 1) & (c - 1))
             - lax.broadcasted_iota(jnp.int32, (c, 2 * c), 0)) * sign
    strict2 = diff2 < 0
    incl2 = diff2 <= 0
    m0 = _lane_mask((c, LANES))
    cm0 = lax.broadcasted_iota(jnp.int32, (c, 2 * c), 1) < c
    bd = (lax.broadcasted_iota(jnp.int32, (LANES, LANES), 0) // HEAD_DIM
          == lax.broadcasted_iota(jnp.int32, (LANES, LANES), 1) // HEAD_DIM)
    lanes = [slice(p * LANES, (p + 1) * LANES) for p in range(npairs)]
    results = _run_lockstep([
        _scan_pair(lw_ref[:, sl], r_ref[:, sl], v_ref[:, sl], kk_ref[:, sl], kd_ref[:, sl], kb_ref[:, sl],
                   s_ref[p], tri, strict2, incl2, m0, cm0, bd)
        for p, sl in enumerate(lanes)])
    for p, (o, s_new) in enumerate(results):
        o_ref[:, lanes[p]] = o
        s_ref[p] = s_new

    @pl.when(i == nchunk - 1)
    def _():
        sout_ref[...] = s_ref[...]


def _rwkv_scan(ops, s0):
    _, b, l, dh = ops.shape
    c = SCAN_CHUNK
    assert l % c == 0
    nchunk = l // c
    npairs = dh // LANES
    tb = lambda d, i: i + d * (nchunk - 1 - 2 * i)
    shared = lambda idx: pl.BlockSpec((None, None, c, dh), lambda bb, d, i: (idx, bb, tb(d, i), 0))
    perdir = lambda idx: pl.BlockSpec((None, None, c, dh), lambda bb, d, i: (idx + d, bb, tb(d, i), 0))
    sspec = pl.BlockSpec((None, None, npairs, LANES, LANES), lambda bb, d, i: (bb, d, 0, 0, 0))
    return pl.pallas_call(
        functools.partial(_scan_kernel, nchunk=nchunk, npairs=npairs),
        grid=(b, 2, nchunk),
        in_specs=[perdir(3), shared(0), shared(1), shared(2), perdir(5), perdir(7), sspec],
        out_specs=[pl.BlockSpec((None, None, c, dh), lambda bb, d, i: (d, bb, tb(d, i), 0)), sspec],
        out_shape=[jax.ShapeDtypeStruct((2, b, l, dh), F32),
                   jax.ShapeDtypeStruct((b, 2, npairs, LANES, LANES), F32)],
        scratch_shapes=[pltpu.VMEM((npairs, LANES, LANES), F32)],
        compiler_params=_params("parallel", "arbitrary", "arbitrary"),
        name="rwkv_scan",
    )(ops, ops, ops, ops, ops, ops, s0)


def _rwkv_post_kernel(of_ref, ob_ref, bonus_ref, gb_ref, g_ref, b_ref, o_ref):
    ones = _head_ones()
    for s in range(o_ref.shape[1] // LANES):
        sl = slice(s * LANES, (s + 1) * LANES)
        o = of_ref[:, sl] + ob_ref[:, sl]
        mu = _head_sum(o, ones) * (1.0 / HEAD_DIM)
        oc = o - mu
        var = _head_sum(oc * oc, ones) * (1.0 / HEAD_DIM)
        o = oc * lax.rsqrt(var + GN_EPS) * g_ref[:, sl] + b_ref[:, sl] + bonus_ref[:, sl]
        o_ref[:, sl] = (o * _silu(gb_ref[:, sl])).astype(BF16)


def _rwkv_post(o2, bonus, z, dh, gn_g, gn_b):
    _, b, l, _ = o2.shape
    tl = _tile(l, 512, SUBLANES)
    tw = _tile(dh, 512, LANES)
    nb = dh // tw
    gate_blk = 7 * nb
    return pl.pallas_call(
        _rwkv_post_kernel,
        grid=(b, l // tl, nb),
        in_specs=[pl.BlockSpec((None, None, tl, tw), lambda bb, i, j: (0, bb, i, j)),
                  pl.BlockSpec((None, None, tl, tw), lambda bb, i, j: (1, bb, i, j)),
                  pl.BlockSpec((None, tl, tw), lambda bb, i, j: (bb, i, j)),
                  pl.BlockSpec((None, tl, tw), lambda bb, i, j: (bb, i, gate_blk + j)),
                  pl.BlockSpec((1, tw), lambda bb, i, j: (0, j)),
                  pl.BlockSpec((1, tw), lambda bb, i, j: (0, j))],
        out_specs=pl.BlockSpec((None, tl, tw), lambda bb, i, j: (bb, i, j)),
        out_shape=jax.ShapeDtypeStruct((b, l, dh), BF16),
        compiler_params=_params("parallel", "parallel", "parallel"),
        name="rwkv_post",
    )(o2, o2, bonus, z, gn_g.reshape(1, dh), gn_b.reshape(1, dh))


def _rwkv_heads(z, dh, pe, s0):
    b, l, _ = z.shape
    nh = dh // HEAD_DIM
    npairs = dh // LANES
    ops, bonus = _rwkv_prep(z, dh, pe["conv_w"], pe["w0"], pe["wup"], pe["a0"], pe["aup"], pe["k_k"], pe["k_a"], pe["r_k"])
    if s0 is None:
        st = jnp.zeros((b, 2, npairs, LANES, LANES), F32)
    else:
        s = s0.astype(F32).reshape(b, 2, npairs, 2, HEAD_DIM, HEAD_DIM)
        zero = jnp.zeros_like(s[:, :, :, 0])
        st = jnp.concatenate([jnp.concatenate([s[:, :, :, 0], zero], axis=-1),
                              jnp.concatenate([zero, s[:, :, :, 1]], axis=-1)], axis=-2)
    o2, s_fin = _rwkv_scan(ops, st)
    s_fin = jnp.stack([s_fin[..., :HEAD_DIM, :HEAD_DIM], s_fin[..., HEAD_DIM:, HEAD_DIM:]], axis=3)
    s_fin = s_fin.reshape(b, 2, nh, HEAD_DIM, HEAD_DIM)
    mixed = _rwkv_post(o2, bonus, z, dh, pe["gn_g"], pe["gn_b"])
    return mixed, s_fin


def _even_layer(x, sc, sh, gt, ln_g, ln_b, alpha, pe, kv_ctx, s0):
    dh = pe["dh"]
    z = _modmm(x, sc, sh, pe["w_in"])
    if kv_ctx is None:
        mixed_a = _ctx_attn(z, dh)
    else:
        mixed_a = _na_attn(z, dh, kv_ctx[0], kv_ctx[1], pe["bias"])
    mixed_b, s_fin = _rwkv_heads(z, dh, pe, s0)
    out = _outln([mixed_a, mixed_b], [pe["w_out"][:dh], pe["w_out"][dh:]], x, gt, ln_g, ln_b, alpha)
    return out, z, s_fin


def _odd_layer(x, sc, sh, gt, ln_g, ln_b, alpha, po, spec, mats):
    c = po["c"]
    b, l, _ = x.shape
    p = _hy_planes(l)
    z = _modmm(x, sc, sh, po["w_in"], planes=p)
    x0, uu, uub = _hy_conv(z, po["conv_w"], po["conv_b"], c)
    mixed = _hy_fftconv(mats, spec, uub, uu, x0, z, po["skip"])
    mixed = mixed.transpose(0, 2, 1, 3).reshape(b, l, c)
    return _outln([mixed], [po["w_out"]], x, gt, ln_g, ln_b, alpha)


def kernel(x_prompt, x_sample, c, cache_a_k, cache_a_v, state_b, c_ctx, w_ada, b_ada, ln_g, ln_b,
           w_in_even, w_out_even, a_rpb, b_conv_w, b_w0, b_wup, b_a0, b_aup, b_kk, b_ka, b_rk,
           b_gn_g, b_gn_b, w_in_odd, w_out_odd, h_conv_w, h_conv_b, h_ffn_w1, h_ffn_b1, h_ffn_w2,
           h_ffn_b2, h_ffn_w3, h_ffn_b3, h_freq, h_ffn_wout, h_skip):
    depth, d, _ = w_ada.shape
    bx, lx, _ = x_prompt.shape
    by, ly, _ = x_sample.shape
    alpha = (2 * depth) ** 0.25
    dh = b_kk.shape[1]
    nh = dh // HEAD_DIM
    p_len = cache_a_k.shape[2]
    c_hy = h_skip.shape[1]

    rows = 1 + by
    rpad = -(-rows // SUBLANES) * SUBLANES
    cond = jnp.pad(jnp.concatenate([c_ctx[None, :], c], axis=0), ((0, rpad - rows), (0, 0)))
    mods = _ada_mod(cond, w_ada, b_ada)

    x = x_prompt
    y = x_sample
    mats_x = _hy_mats(lx) if depth > 1 else None
    mats_y = _hy_mats(ly) if depth > 1 else None
    new_k, new_v, new_s = [], [], []
    for l in range(depth):
        m = mods[l]
        sh_x, sc_x, gt_x = (jnp.broadcast_to(m[0:1, k * d:(k + 1) * d][None], (bx, 1, d)) for k in range(3))
        sh_y, sc_y, gt_y = (m[1:rows, k * d:(k + 1) * d][:, None, :] for k in range(3))
        if l % 2 == 0:
            e = l // 2
            pe = dict(dh=dh, w_in=w_in_even[e].astype(BF16), w_out=w_out_even[e].astype(BF16),
                      bias=_na_bias_table(a_rpb[e]), conv_w=b_conv_w[e], w0=b_w0[e], wup=b_wup[e], a0=b_a0[e],
                      aup=b_aup[e], k_k=b_kk[e], k_a=b_ka[e], r_k=b_rk[e], gn_g=b_gn_g[e], gn_b=b_gn_b[e])
            x, zx, s_fin = _even_layer(x, sc_x, sh_x, gt_x, ln_g[l], ln_b[l], alpha, pe, None, None)
            new_k.append(zx[..., dh:2 * dh].reshape(bx, lx, nh, HEAD_DIM))
            new_v.append(zx[..., 2 * dh:3 * dh].reshape(bx, lx, nh, HEAD_DIM))
            new_s.append(s_fin)
            kv = (cache_a_k[:, e].reshape(by, p_len, dh), cache_a_v[:, e].reshape(by, p_len, dh))
            y, _, _ = _even_layer(y, sc_y, sh_y, gt_y, ln_g[l], ln_b[l], alpha, pe, kv, state_b[:, e])
        else:
            o = l // 2
            po = dict(c=c_hy, w_in=w_in_odd[o].astype(BF16), w_out=w_out_odd[o].astype(BF16),
                      conv_w=h_conv_w[o], conv_b=h_conv_b[o], skip=h_skip[o])
            fargs = (h_ffn_w1[o], h_ffn_b1[o], h_ffn_w2[o], h_ffn_b2[o], h_ffn_w3[o], h_ffn_b3[o],
                     h_freq[o], h_ffn_wout[o])
            spec_x = _hy_spectrum(lx, c_hy, mats_x[0], *fargs)
            spec_y = _hy_spectrum(ly, c_hy, mats_y[0], *fargs)
            x = _odd_layer(x, sc_x, sh_x, gt_x, ln_g[l], ln_b[l], alpha, po, spec_x, mats_x)
            y = _odd_layer(y, sc_y, sh_y, gt_y, ln_g[l], ln_b[l], alpha, po, spec_y, mats_y)
    return (x, y, jnp.stack(new_k, axis=1), jnp.stack(new_v, axis=1), jnp.stack(new_s, axis=1))
```
